```python
import math, functools
import jax, jax.numpy as jnp
from jax import lax
import numpy as np

D_MODEL = 4096
BATCH = 4
SEQ = 2048
DEPTH = 4
DEC_BATCH = 128
DEC_SEQ = 1
PAST_LEN = 8192
PAGE_SIZE = 128

HEAD_DIM = 128
MOBA_HEADS = D_MODEL // (2 * HEAD_DIM)
MOBA_BLOCK = 256
MOBA_TOPK = 3
MLA_HEADS = D_MODEL // (2 * HEAD_DIM)
MLA_Q_LORA = 3 * D_MODEL // 16
MLA_KV_LORA = D_MODEL // 16
MLA_NOPE = 128
MLA_ROPE = 64
MLA_V = 128
SB_HEADS = D_MODEL // (2 * HEAD_DIM)
DSA_HEADS = D_MODEL // (2 * HEAD_DIM)
IDX_HEADS = D_MODEL // 128
IDX_DIM = 64
DSA_TOPK = 256
D_FF = D_MODEL
N_EVEN = DEPTH // 2
N_ODD = DEPTH - N_EVEN
ROPE_THETA = 10000.0
EPS = 1e-6
Q_BLOCK = 128
MOBA_Q_BLOCK = 16
ATTN_SCALE = HEAD_DIM ** -0.5
MLA_SCALE = (MLA_NOPE + MLA_ROPE) ** -0.5
IDX_SCALE = (IDX_HEADS * IDX_DIM) ** -0.5
EVEN_IN_SIZES = (MOBA_HEADS * HEAD_DIM, HEAD_DIM, HEAD_DIM, MLA_Q_LORA, MLA_KV_LORA, MLA_ROPE)
ODD_IN_SIZES = (SB_HEADS * HEAD_DIM, HEAD_DIM, HEAD_DIM, DSA_HEADS * HEAD_DIM, HEAD_DIM, HEAD_DIM,
                IDX_HEADS * IDX_DIM, IDX_HEADS, IDX_DIM)
EVEN_IN = sum(EVEN_IN_SIZES)
ODD_IN = sum(ODD_IN_SIZES)
EVEN_OUT = MOBA_HEADS * HEAD_DIM + MLA_HEADS * MLA_V
ODD_OUT = SB_HEADS * HEAD_DIM + DSA_HEADS * HEAD_DIM

kernel_name = 'hybrid_moba_mla_stickbreak_dsa_decode_step'


def _rmsnorm(x, w):
    x32 = x.astype(jnp.float32)
    y = x32 * lax.rsqrt(jnp.mean(x32 * x32, axis=-1, keepdims=True) + EPS)
    return (y * w.astype(jnp.float32)).astype(x.dtype)


def _rope(x, pos):
    d = x.shape[-1]
    half = d // 2
    inv_freq = ROPE_THETA ** (-2.0 * jnp.arange(half, dtype=jnp.float32) / d)
    ang = pos.astype(jnp.float32)[:, None] * inv_freq[None, :]
    shape = (1, pos.shape[0]) + (1,) * (x.ndim - 3) + (half,)
    cos = jnp.cos(ang).reshape(shape)
    sin = jnp.sin(ang).reshape(shape)
    x32 = x.astype(jnp.float32)
    x1, x2 = x32[..., :half], x32[..., half:]
    return jnp.concatenate([x1 * cos - x2 * sin, x2 * cos + x1 * sin], axis=-1).astype(x.dtype)


def _split(x, sizes):
    offs, acc = [], 0
    for s in sizes[:-1]:
        acc += s
        offs.append(acc)
    return jnp.split(x, offs, axis=-1)


def _swiglu(h, w_gate, w_up, w_down):
    return (jax.nn.silu(h @ w_gate) * (h @ w_up)) @ w_down


def _gather_pages(cache, layer, page_table):
    rows = cache[layer, page_table]
    return rows.reshape(rows.shape[0], rows.shape[1] * rows.shape[2], *rows.shape[3:])


def _sweep_queries(fn, block, pos, *qs):
    T = pos.shape[0]
    blk = math.gcd(T, block)
    if blk == T:
        return fn(pos, *qs)
    nb = T // blk

    def split(a):
        return jnp.moveaxis(a.reshape(a.shape[0], nb, blk, *a.shape[2:]), 1, 0)

    out = lax.map(lambda args: fn(*args), (pos.reshape(nb, blk),) + tuple(split(q) for q in qs))
    out = jnp.moveaxis(out, 0, 1)
    return out.reshape(out.shape[0], T, *out.shape[3:])


def _moba_attention(q, k, v, q_pos):
    B, L, D = k.shape
    nb = -(-L // MOBA_BLOCK)
    pad = nb * MOBA_BLOCK - L
    k_blk = jnp.pad(k, ((0, 0), (0, pad), (0, 0))).reshape(B, nb, MOBA_BLOCK, D)
    v_blk = jnp.pad(v, ((0, 0), (0, pad), (0, 0))).reshape(B, nb, MOBA_BLOCK, D)
    k_mean = jnp.mean(k_blk.astype(jnp.float32), axis=2)
    topk = min(MOBA_TOPK, nb)
    blk_ids = jnp.arange(nb)
    in_blk = jnp.arange(MOBA_BLOCK)

    def block(pos, qb):
        own = pos // MOBA_BLOCK
        gate = jnp.einsum('bqhd,bnd->bqhn', qb.astype(jnp.float32), k_mean)
        past_ok = (blk_ids[None, :] < own[:, None])[None, :, None, :]
        gate = jnp.where(past_ok, gate, -jnp.inf)
        _, sel = lax.top_k(gate, topk)
        sel_ok = sel < own[None, :, None, None]
        k_sel = jax.vmap(lambda kb, s: kb[s])(k_blk, sel)
        v_sel = jax.vmap(lambda vb, s: vb[s])(v_blk, sel)
        s_past = jnp.einsum('bqhd,bqhnjd->bqhnj', qb, k_sel, preferred_element_type=jnp.float32) * ATTN_SCALE
        s_past = jnp.where(sel_ok[..., None], s_past, -jnp.inf)
        Bq, Tq, H = s_past.shape[:3]
        s_past = s_past.reshape(Bq, Tq, H, topk * MOBA_BLOCK)
        k_own = k_blk[:, own]
        v_own = v_blk[:, own]
        own_pos = own[:, None] * MOBA_BLOCK + in_blk[None, :]
        s_own = jnp.einsum('bqhd,bqjd->bqhj', qb, k_own, preferred_element_type=jnp.float32) * ATTN_SCALE
        s_own = jnp.where((own_pos <= pos[:, None])[None, :, None, :], s_own, -jnp.inf)
        p = jax.nn.softmax(jnp.concatenate([s_past, s_own], axis=-1), axis=-1)
        p_past = p[..., :topk * MOBA_BLOCK].reshape(Bq, Tq, H, topk, MOBA_BLOCK).astype(v.dtype)
        p_own = p[..., topk * MOBA_BLOCK:].astype(v.dtype)
        return (jnp.einsum('bqhnj,bqhnjd->bqhd', p_past, v_sel)
                + jnp.einsum('bqhj,bqjd->bqhd', p_own, v_own))

    return _sweep_queries(block, MOBA_Q_BLOCK, q_pos, q)


def _mla_attention(q, kv, q_pos):
    L = kv.shape[1]
    k_pos = jnp.arange(L)
    v = kv[..., :MLA_KV_LORA]

    def block(pos, qb):
        s = jnp.einsum('bqhc,bkc->bqhk', qb, kv, preferred_element_type=jnp.float32) * MLA_SCALE
        s = jnp.where((k_pos[None, :] <= pos[:, None])[None, :, None, :], s, -jnp.inf)
        p = jax.nn.softmax(s, axis=-1)
        return jnp.einsum('bqhk,bkc->bqhc', p.astype(v.dtype), v)

    return _sweep_queries(block, Q_BLOCK, q_pos, q)


def _stick_breaking(q, k, v, q_pos):
    L = k.shape[1]
    k_pos = jnp.arange(L)

    def block(pos, qb):
        z = jnp.einsum('bqhd,bkd->bqhk', qb, k, preferred_element_type=jnp.float32) * ATTN_SCALE
        strict = (k_pos[None, :] < pos[:, None])[None, :, None, :]
        log_keep = jnp.where(strict, jax.nn.log_sigmoid(-z), 0.0)
        later = lax.cumsum(log_keep, axis=3, reverse=True) - log_keep
        a = jnp.where(strict, jnp.exp(jax.nn.log_sigmoid(z) + later), 0.0)
        return jnp.einsum('bqhk,bkd->bqhd', a.astype(v.dtype), v)

    return _sweep_queries(block, Q_BLOCK, q_pos, q)


def _dsa_attention(q, k, v, q_idx, w_idx, k_idx, q_pos):
    L = k.shape[1]
    n_sel = min(DSA_TOPK, L // 4)
    k_pos = jnp.arange(L)

    def block(pos, qb, qib, wib):
        logits = jnp.einsum('bqhd,bkd->bqhk', qib, k_idx, preferred_element_type=jnp.float32)
        score = jnp.einsum('bqh,bqhk->bqk', wib.astype(jnp.float32) * IDX_SCALE, jax.nn.relu(logits))
        score = jnp.where((k_pos[None, :] <= pos[:, None])[None], score, -jnp.inf)
        _, idx = lax.top_k(score, n_sel)
        ok = idx <= pos[None, :, None]
        k_g = jax.vmap(lambda kk, ii: kk[ii])(k, idx)
        v_g = jax.vmap(lambda vv, ii: vv[ii])(v, idx)
        s = jnp.einsum('bqhd,bqjd->bqhj', qb, k_g, preferred_element_type=jnp.float32) * ATTN_SCALE
        s = jnp.where(ok[:, :, None, :], s, -jnp.inf)
        p = jax.nn.softmax(s, axis=-1)
        return jnp.einsum('bqhj,bqjd->bqhd', p.astype(v.dtype), v_g)

    return _sweep_queries(block, Q_BLOCK, q_pos, q, q_idx, w_idx)


def _even_mixer(h, pos, past, w_in, w_out, mla_q_norm_w, mla_w_uq, mla_kv_norm_w, mla_w_uk, mla_w_uv):
    B, T, _ = h.shape
    mq, mk, mv, cq, ckv, kr = _split(h @ w_in, EVEN_IN_SIZES)
    mq = _rope(mq.reshape(B, T, MOBA_HEADS, HEAD_DIM), pos)
    new_moba = jnp.stack([_rope(mk, pos), mv], axis=2)
    q = (_rmsnorm(cq, mla_q_norm_w) @ mla_w_uq).reshape(B, T, MLA_HEADS, MLA_NOPE + MLA_ROPE)
    q_lat = jnp.einsum('bthn,chn->bthc', q[..., :MLA_NOPE], mla_w_uk)
    q_cat = jnp.concatenate([q_lat, _rope(q[..., MLA_NOPE:], pos)], axis=-1)
    new_mla = jnp.concatenate([_rmsnorm(ckv, mla_kv_norm_w), _rope(kr, pos)], axis=-1)
    if past is None:
        moba_all, mla_all = new_moba, new_mla
    else:
        moba_all = jnp.concatenate([past[0], new_moba], axis=1)
        mla_all = jnp.concatenate([past[1], new_mla], axis=1)
    moba_out = _moba_attention(mq, moba_all[:, :, 0], moba_all[:, :, 1], pos)
    mla_out = jnp.einsum('bthc,chv->bthv', _mla_attention(q_cat, mla_all, pos), mla_w_uv)
    mixed = jnp.concatenate([moba_out.reshape(B, T, -1), mla_out.reshape(B, T, -1)], axis=-1)
    return (mixed @ w_out).astype(h.dtype), (new_moba, new_mla)


def _odd_mixer(h, pos, past, w_in, w_out):
    B, T, _ = h.shape
    sq, sk, sv, dq, dk, dv, iq, iw, ik = _split(h @ w_in, ODD_IN_SIZES)
    sq = sq.reshape(B, T, SB_HEADS, HEAD_DIM)
    dq = _rope(dq.reshape(B, T, DSA_HEADS, HEAD_DIM), pos)
    iq = _rope(iq.reshape(B, T, IDX_HEADS, IDX_DIM), pos)
    new_sb = jnp.stack([sk, sv], axis=2)
    new_dsa = jnp.stack([_rope(dk, pos), dv], axis=2)
    new_idx = _rope(ik, pos)
    if past is None:
        sb_all, dsa_all, idx_all = new_sb, new_dsa, new_idx
    else:
        sb_all = jnp.concatenate([past[0], new_sb], axis=1)
        dsa_all = jnp.concatenate([past[1], new_dsa], axis=1)
        idx_all = jnp.concatenate([past[2], new_idx], axis=1)
    sb_out = _stick_breaking(sq, sb_all[:, :, 0], sb_all[:, :, 1], pos)
    dsa_out = _dsa_attention(dq, dsa_all[:, :, 0], dsa_all[:, :, 1], iq, iw, idx_all, pos)
    mixed = jnp.concatenate([sb_out.reshape(B, T, -1), dsa_out.reshape(B, T, -1)], axis=-1)
    return (mixed @ w_out).astype(h.dtype), (new_sb, new_dsa, new_idx)


def _layer(x, pos, past, mixer, norm_w, w_gate, w_up, w_down):
    x = x + 0.5 * _swiglu(_rmsnorm(x, norm_w[0]), w_gate[0], w_up[0], w_down[0])
    m, states = mixer(_rmsnorm(x, norm_w[1]), pos, past)
    x = x + m
    x = x + 0.5 * _swiglu(_rmsnorm(x, norm_w[2]), w_gate[1], w_up[1], w_down[1])
    return x, states


def setup_inputs(seed: int = 0) -> dict:
    key = jax.random.key(seed)
    ks = iter(jax.random.split(key, 32))
    n_pages = PAST_LEN // PAGE_SIZE
    n_used = DEC_BATCH * n_pages
    n_pool = n_used + max(1, n_used // 4)

    def rnd(shape, scale=None):
        r = jax.random.normal(next(ks), shape, jnp.float32)
        return r if scale is None else r * scale

    def gain(shape):
        return 1.0 + 0.02 * jax.random.normal(next(ks), shape, jnp.float32)

    page_table = jax.random.permutation(next(ks), n_pool)[:n_used].reshape(DEC_BATCH, n_pages).astype(jnp.int32)
    return {
        'x_prompt': rnd((BATCH, SEQ, D_MODEL)),
        'x_sample': rnd((DEC_BATCH, DEC_SEQ, D_MODEL)),
        'cache_moba_kv': rnd((N_EVEN, n_pool, PAGE_SIZE, 2, HEAD_DIM)),
        'cache_mla_latent': rnd((N_EVEN, n_pool, PAGE_SIZE, MLA_KV_LORA + MLA_ROPE)),
        'cache_sb_kv': rnd((N_ODD, n_pool, PAGE_SIZE, 2, HEAD_DIM)),
        'cache_dsa_kv': rnd((N_ODD, n_pool, PAGE_SIZE, 2, HEAD_DIM)),
        'cache_dsa_idx_k': rnd((N_ODD, n_pool, PAGE_SIZE, IDX_DIM)),
        'page_table': page_table,
        'norm_w': gain((DEPTH, 3, D_MODEL)),
        'ffn_w_gate': rnd((DEPTH, 2, D_MODEL, D_FF), D_MODEL ** -0.5),
        'ffn_w_up': rnd((DEPTH, 2, D_MODEL, D_FF), D_MODEL ** -0.5),
        'ffn_w_down': rnd((DEPTH, 2, D_FF, D_MODEL), D_FF ** -0.5),
        'even_w_in': rnd((N_EVEN, D_MODEL, EVEN_IN), D_MODEL ** -0.5),
        'even_w_out': rnd((N_EVEN, EVEN_OUT, D_MODEL), EVEN_OUT ** -0.5),
        'mla_q_norm_w': gain((N_EVEN, MLA_Q_LORA)),
        'mla_w_uq': rnd((N_EVEN, MLA_Q_LORA, MLA_HEADS * (MLA_NOPE + MLA_ROPE)), MLA_Q_LORA ** -0.5),
        'mla_kv_norm_w': gain((N_EVEN, MLA_KV_LORA)),
        'mla_w_uk': rnd((N_EVEN, MLA_KV_LORA, MLA_HEADS, MLA_NOPE), MLA_KV_LORA ** -0.5),
        'mla_w_uv': rnd((N_EVEN, MLA_KV_LORA, MLA_HEADS, MLA_V), MLA_KV_LORA ** -0.5),
        'odd_w_in': rnd((N_ODD, D_MODEL, ODD_IN), D_MODEL ** -0.5),
        'odd_w_out': rnd((N_ODD, ODD_OUT, D_MODEL), ODD_OUT ** -0.5),
        'final_norm_w': gain((D_MODEL,)),
    }


def reference(x_prompt, x_sample, cache_moba_kv, cache_mla_latent, cache_sb_kv, cache_dsa_kv, cache_dsa_idx_k,
              page_table, norm_w, ffn_w_gate, ffn_w_up, ffn_w_down, even_w_in, even_w_out, mla_q_norm_w,
              mla_w_uq, mla_kv_norm_w, mla_w_uk, mla_w_uv, odd_w_in, odd_w_out, final_norm_w):
    pos_p = jnp.arange(SEQ, dtype=jnp.int32)
    pos_s = PAST_LEN + jnp.arange(DEC_SEQ, dtype=jnp.int32)
    yp, ys = x_prompt, x_sample
    moba_p, moba_s, mla_p, mla_s = [], [], [], []
    sb_p, sb_s, dsa_p, dsa_s, idx_p, idx_s = [], [], [], [], [], []
    for layer in range(DEPTH):
        j = layer // 2
        if layer % 2 == 0:
            mixer = functools.partial(_even_mixer, w_in=even_w_in[j], w_out=even_w_out[j],
                                      mla_q_norm_w=mla_q_norm_w[j], mla_w_uq=mla_w_uq[j],
                                      mla_kv_norm_w=mla_kv_norm_w[j], mla_w_uk=mla_w_uk[j], mla_w_uv=mla_w_uv[j])
            past = (_gather_pages(cache_moba_kv, j, page_table), _gather_pages(cache_mla_latent, j, page_table))
        else:
            mixer = functools.partial(_odd_mixer, w_in=odd_w_in[j], w_out=odd_w_out[j])
            past = (_gather_pages(cache_sb_kv, j, page_table), _gather_pages(cache_dsa_kv, j, page_table),
                    _gather_pages(cache_dsa_idx_k, j, page_table))
        ffn = (norm_w[layer], ffn_w_gate[layer], ffn_w_up[layer], ffn_w_down[layer])
        yp, st_p = _layer(yp, pos_p, None, mixer, *ffn)
        ys, st_s = _layer(ys, pos_s, past, mixer, *ffn)
        if layer % 2 == 0:
            moba_p.append(st_p[0]); mla_p.append(st_p[1])
            moba_s.append(st_s[0]); mla_s.append(st_s[1])
        else:
            sb_p.append(st_p[0]); dsa_p.append(st_p[1]); idx_p.append(st_p[2])
            sb_s.append(st_s[0]); dsa_s.append(st_s[1]); idx_s.append(st_s[2])
    y_prompt = _rmsnorm(yp, final_norm_w)
    y_sample = _rmsnorm(ys, final_norm_w)
    return (y_prompt, y_sample,
            jnp.stack(moba_p), jnp.stack(moba_s),
            jnp.stack(mla_p), jnp.stack(mla_s),
            jnp.stack(sb_p), jnp.stack(sb_s),
            jnp.stack(dsa_p), jnp.stack(dsa_s),
            jnp.stack(idx_p), jnp.stack(idx_s))
```

```python
import functools

import jax
import jax.numpy as jnp
from jax import lax
from jax.experimental import pallas as pl
from jax.experimental.pallas import tpu as pltpu

F32 = jnp.float32
BF16 = jnp.bfloat16
I32 = jnp.int32

D_MODEL = 4096
HEAD_DIM = 128
N_HEADS = 16
MOBA_BLOCK = 256
MOBA_TOPK = 3
MLA_Q_LORA = 768
MLA_KV_LORA = 256
MLA_ROPE = 64
MLA_QK = 192
IDX_HEADS = 32
IDX_DIM = 64
DSA_TOPK = 256
PAGE_SIZE = 128
ROPE_THETA = 10000.0
EPS = 1e-6
ATTN_SCALE = HEAD_DIM ** -0.5
MLA_SCALE = MLA_QK ** -0.5
IDX_SCALE = (IDX_HEADS * IDX_DIM) ** -0.5
NEG = -1e30
INT_MIN = -2 ** 31

EVEN_PAD = 3456
ODD_PAD = 6912
MLA_KV_PAD = 384
QT = 256
VMEM_LIMIT_MB = 52


def _cparams(sem, vmem_mb=None):
    return pltpu.CompilerParams(
        dimension_semantics=sem,
        vmem_limit_bytes=None if vmem_mb is None else vmem_mb * 2 ** 20)


def _dot(a, b):
    return jnp.dot(a, b, preferred_element_type=F32)


def _dot_nt(a, b):
    return lax.dot_general(a, b, (((1,), (1,)), ((), ())), preferred_element_type=F32)


def _split_bf16(x):
    hi = x.astype(BF16)
    lo = (x - hi.astype(F32)).astype(BF16)
    return hi, lo


def _dot3_nt(a, b):
    ah, al = _split_bf16(a)
    bh, bl = _split_bf16(b)
    return _dot_nt(ah, bh) + (_dot_nt(ah, bl) + _dot_nt(al, bh))


def _softplus(z):
    return jnp.maximum(z, 0.0) + jnp.log(1.0 + jnp.exp(-jnp.abs(z)))


def _rope128(x, cos, sin_signed):
    return x * cos + pltpu.roll(x, 64, 1) * sin_signed


def _rope64(x, cos, sin_a, sin_b):
    return x * cos + pltpu.roll(x, 96, 1) * sin_a + pltpu.roll(x, 32, 1) * sin_b


def _topk_rows(g, k):
    col = lax.broadcasted_iota(I32, g.shape, 1).astype(F32)
    big = jnp.float32(g.shape[1])
    sel = jnp.zeros(g.shape, F32)
    for _ in range(k):
        mx = jnp.max(g, axis=1, keepdims=True)
        cand = jnp.where(g == mx, col, big)
        cand = jnp.where(mx > -jnp.inf, cand, big)
        first = jnp.min(cand, axis=1, keepdims=True)
        pick = col == first
        sel = jnp.where(pick, 1.0, sel)
        g = jnp.where(pick, -jnp.inf, g)
    return sel


def _rmsnorm_body(x_ref, w_ref, o_ref):
    x = x_ref[...]
    ms = jnp.mean(x * x, axis=-1, keepdims=True)
    o_ref[...] = ((x * lax.rsqrt(ms + EPS)) * w_ref[...]).astype(o_ref.dtype)


def rmsnorm(x, w, out_dtype, tm):
    m, d = x.shape
    return pl.pallas_call(
        _rmsnorm_body,
        out_shape=jax.ShapeDtypeStruct((m, d), out_dtype),
        grid=(m // tm,),
        in_specs=[pl.BlockSpec((tm, d), lambda i: (i, 0)),
                  pl.BlockSpec((1, d), lambda i: (0, 0))],
        out_specs=pl.BlockSpec((tm, d), lambda i: (i, 0)),
        compiler_params=_cparams(("parallel",), VMEM_LIMIT_MB),
        name="rmsnorm",
    )(x, w.reshape(1, d))


def _mm_body(a_ref, b_ref, o_ref):
    o_ref[...] = _dot(a_ref[...].astype(BF16), b_ref[...].astype(BF16)).astype(o_ref.dtype)


def matmul(a, b, out_dtype, tm, tn):
    m, k = a.shape
    n = b.shape[1]
    return pl.pallas_call(
        _mm_body,
        out_shape=jax.ShapeDtypeStruct((m, n), out_dtype),
        grid=(m // tm, n // tn),
        in_specs=[pl.BlockSpec((tm, k), lambda i, j: (i, 0)),
                  pl.BlockSpec((k, tn), lambda i, j: (0, j))],
        out_specs=pl.BlockSpec((tm, tn), lambda i, j: (i, j)),
        compiler_params=_cparams(("parallel", "parallel"), VMEM_LIMIT_MB),
        name="matmul",
    )(a, b)


def _weight_spec(w, lead, tn):
    k = w.shape[-2]
    return pl.BlockSpec((None,) * len(lead) + (k, tn), lambda i, j: tuple(lead) + (0, j))


def _ffn_up_body(a_ref, g_ref, u_ref, o_ref):
    a = a_ref[...]
    g = _dot(a, g_ref[...].astype(BF16))
    u = _dot(a, u_ref[...].astype(BF16))
    o_ref[...] = ((g * (1.0 / (1.0 + jnp.exp(-g)))) * u).astype(o_ref.dtype)


def ffn_up(h, w_gate, w_up, lead, tm, tn):
    m, k = h.shape
    n = w_gate.shape[-1]
    return pl.pallas_call(
        _ffn_up_body,
        out_shape=jax.ShapeDtypeStruct((m, n), BF16),
        grid=(m // tm, n // tn),
        in_specs=[pl.BlockSpec((tm, k), lambda i, j: (i, 0)),
                  _weight_spec(w_gate, lead, tn),
                  _weight_spec(w_up, lead, tn)],
        out_specs=pl.BlockSpec((tm, tn), lambda i, j: (i, j)),
        compiler_params=_cparams(("parallel", "parallel"), VMEM_LIMIT_MB),
        name="ffn_up",
    )(h, w_gate, w_up)


def _mm_res_body(a_ref, b_ref, x_ref, o_ref, *, scale):
    acc = _dot(a_ref[...], b_ref[...].astype(BF16))
    o_ref[...] = x_ref[...] + scale * acc


def matmul_residual(a, b, lead, x, scale, tm, tn):
    m, k = a.shape
    n = b.shape[-1]
    return pl.pallas_call(
        functools.partial(_mm_res_body, scale=scale),
        out_shape=jax.ShapeDtypeStruct((m, n), F32),
        grid=(m // tm, n // tn),
        in_specs=[pl.BlockSpec((tm, k), lambda i, j: (i, 0)),
                  _weight_spec(b, lead, tn),
                  pl.BlockSpec((tm, tn), lambda i, j: (i, j))],
        out_specs=pl.BlockSpec((tm, tn), lambda i, j: (i, j)),
        compiler_params=_cparams(("parallel", "parallel"), VMEM_LIMIT_MB),
        name="matmul_residual",
    )(a, b, x)


def _mm2_res_body(a1_ref, a2_ref, b_ref, x_ref, o_ref):
    k1 = a1_ref.shape[1]
    acc = _dot(a1_ref[...], b_ref[:k1, :].astype(BF16)) + _dot(a2_ref[...], b_ref[k1:, :].astype(BF16))
    o_ref[...] = x_ref[...] + acc


def matmul2_residual(a1, a2, b, lead, x, tm, tn):
    m, k1 = a1.shape
    k2 = a2.shape[1]
    n = b.shape[-1]
    return pl.pallas_call(
        _mm2_res_body,
        out_shape=jax.ShapeDtypeStruct((m, n), F32),
        grid=(m // tm, n // tn),
        in_specs=[pl.BlockSpec((tm, k1), lambda i, j: (i, 0)),
                  pl.BlockSpec((tm, k2), lambda i, j: (i, 0)),
                  _weight_spec(b, lead, tn),
                  pl.BlockSpec((tm, tn), lambda i, j: (i, j))],
        out_specs=pl.BlockSpec((tm, tn), lambda i, j: (i, j)),
        compiler_params=_cparams(("parallel", "parallel"), VMEM_LIMIT_MB),
        name="matmul2_residual",
    )(a1, a2, b, x)


def _headwise_body(x_ref, w_ref, o_ref):
    o_ref[...] = _dot(x_ref[...].astype(BF16), w_ref[0]).astype(o_ref.dtype)


def headwise_matmul(x, w, out_dtype, tm):
    m = x.shape[0]
    nh, din, dout = w.shape
    return pl.pallas_call(
        _headwise_body,
        out_shape=jax.ShapeDtypeStruct((m, nh * dout), out_dtype),
        grid=(m // tm, nh),
        in_specs=[pl.BlockSpec((tm, din), lambda i, h: (i, h)),
                  pl.BlockSpec((1, din, dout), lambda i, h: (h, 0, 0))],
        out_specs=pl.BlockSpec((tm, dout), lambda i, h: (i, h)),
        compiler_params=_cparams(("parallel", "parallel"), VMEM_LIMIT_MB),
        name="headwise_matmul",
    )(x, w)


def _mla_qcat_body(qn_ref, qr_ref, w_ref, c_ref, sa_ref, sb_ref, o_ref):
    lat = _dot(qn_ref[...].astype(BF16), w_ref[0])
    rot = _rope64(qr_ref[...], c_ref[...], sa_ref[...], sb_ref[...])
    o_ref[:, :MLA_KV_LORA] = lat.astype(o_ref.dtype)
    o_ref[:, MLA_KV_LORA:] = rot.astype(o_ref.dtype)


def mla_qcat(q, w_uk_t, tabs64, tm):
    m = q.shape[0]
    c64, sa, sb = tabs64
    tab = pl.BlockSpec((tm, 128), lambda i, h: (i, 0))
    return pl.pallas_call(
        _mla_qcat_body,
        out_shape=jax.ShapeDtypeStruct((m, N_HEADS * MLA_KV_PAD), BF16),
        grid=(m // tm, N_HEADS),
        in_specs=[pl.BlockSpec((tm, 128), lambda i, h: (i, h)),
                  pl.BlockSpec((tm, 128), lambda i, h: (i, N_HEADS + h)),
                  pl.BlockSpec((1, 128, MLA_KV_LORA), lambda i, h: (h, 0, 0)),
                  tab, tab, tab],
        out_specs=pl.BlockSpec((tm, MLA_KV_PAD), lambda i, h: (i, h)),
        compiler_params=_cparams(("parallel", "parallel"), VMEM_LIMIT_MB),
        name="mla_qcat",
    )(q, q, w_uk_t, c64, sa, sb)


def _even_post_body(p_ref, cos_ref, sin_ref, c64_ref, sa_ref, sb_ref, qnw_ref, kvnw_ref,
                    mq_ref, moba_ref, cqn_ref, mla_ref, kvp_ref):
    cos = cos_ref[...]
    sin = sin_ref[...]
    for h in range(N_HEADS):
        sl = slice(h * 128, (h + 1) * 128)
        mq_ref[:, sl] = _rope128(p_ref[:, sl], cos, sin)
    moba_ref[:, 0:128] = _rope128(p_ref[:, 2048:2176], cos, sin)
    moba_ref[:, 128:256] = p_ref[:, 2176:2304]
    cq = p_ref[:, 2304:3072]
    ms = jnp.mean(cq * cq, axis=-1, keepdims=True)
    cqn_ref[...] = ((cq * lax.rsqrt(ms + EPS)) * qnw_ref[...]).astype(cqn_ref.dtype)
    ckv = p_ref[:, 3072:3328]
    ms = jnp.mean(ckv * ckv, axis=-1, keepdims=True)
    ckvn = (ckv * lax.rsqrt(ms + EPS)) * kvnw_ref[...]
    kr = _rope64(p_ref[:, 3328:3456], c64_ref[...], sa_ref[...], sb_ref[...])
    mla_ref[:, 0:MLA_KV_LORA] = ckvn
    mla_ref[:, MLA_KV_LORA:MLA_KV_LORA + MLA_ROPE] = kr[:, 0:MLA_ROPE]
    kvp_ref[:, 0:MLA_KV_LORA] = ckvn.astype(kvp_ref.dtype)
    kvp_ref[:, MLA_KV_LORA:] = kr.astype(kvp_ref.dtype)


def even_post(p, tabs128, tabs64, q_norm_w, kv_norm_w, tm):
    m = p.shape[0]
    row = lambda w: pl.BlockSpec((tm, w), lambda i: (i, 0))
    return pl.pallas_call(
        _even_post_body,
        out_shape=(jax.ShapeDtypeStruct((m, 2048), F32),
                   jax.ShapeDtypeStruct((m, 256), F32),
                   jax.ShapeDtypeStruct((m, MLA_Q_LORA), BF16),
                   jax.ShapeDtypeStruct((m, MLA_KV_LORA + MLA_ROPE), F32),
                   jax.ShapeDtypeStruct((m, MLA_KV_PAD), BF16)),
        grid=(m // tm,),
        in_specs=[row(EVEN_PAD), row(128), row(128), row(128), row(128), row(128),
                  pl.BlockSpec((1, MLA_Q_LORA), lambda i: (0, 0)),
                  pl.BlockSpec((1, MLA_KV_LORA), lambda i: (0, 0))],
        out_specs=(row(2048), row(256), row(MLA_Q_LORA), row(MLA_KV_LORA + MLA_ROPE), row(MLA_KV_PAD)),
        compiler_params=_cparams(("parallel",), VMEM_LIMIT_MB),
        name="even_post",
    )(p, *tabs128, *tabs64, q_norm_w.reshape(1, -1), kv_norm_w.reshape(1, -1))


def _odd_post_body(p_ref, cos_ref, sin_ref, c64_ref, sa_ref, sb_ref,
                   sq_ref, sb_new_ref, dq_ref, dsa_new_ref, iq_ref, iw_ref, idx_new_ref, ikd_ref):
    cos = cos_ref[...]
    sin = sin_ref[...]
    c64 = c64_ref[...]
    sa = sa_ref[...]
    sb = sb_ref[...]
    sq_ref[...] = p_ref[:, 0:2048]
    sb_new_ref[...] = p_ref[:, 2048:2304]
    for h in range(N_HEADS):
        sl = slice(h * 128, (h + 1) * 128)
        dq_ref[:, sl] = _rope128(p_ref[:, 2304 + h * 128:2304 + (h + 1) * 128], cos, sin)
        iq_ref[:, sl] = _rope64(p_ref[:, 4608 + h * 128:4608 + (h + 1) * 128], c64, sa, sb).astype(iq_ref.dtype)
    dsa_new_ref[:, 0:128] = _rope128(p_ref[:, 4352:4480], cos, sin)
    dsa_new_ref[:, 128:256] = p_ref[:, 4480:4608]
    iw_ref[...] = p_ref[:, 6656:6784] * IDX_SCALE
    ik = _rope64(p_ref[:, 6784:6912], c64, sa, sb)
    idx_new_ref[...] = ik[:, 0:IDX_DIM]
    ikd_ref[...] = (ik + pltpu.roll(ik, 64, 1)).astype(ikd_ref.dtype)


def odd_post(p, tabs128, tabs64, tm):
    m = p.shape[0]
    row = lambda w: pl.BlockSpec((tm, w), lambda i: (i, 0))
    return pl.pallas_call(
        _odd_post_body,
        out_shape=(jax.ShapeDtypeStruct((m, 2048), F32),
                   jax.ShapeDtypeStruct((m, 256), F32),
                   jax.ShapeDtypeStruct((m, 2048), F32),
                   jax.ShapeDtypeStruct((m, 256), F32),
                   jax.ShapeDtypeStruct((m, 2048), BF16),
                   jax.ShapeDtypeStruct((m, 128), F32),
                   jax.ShapeDtypeStruct((m, IDX_DIM), F32),
                   jax.ShapeDtypeStruct((m, 128), BF16)),
        grid=(m // tm,),
        in_specs=[row(ODD_PAD), row(128), row(128), row(128), row(128), row(128)],
        out_specs=(row(2048), row(256), row(2048), row(256), row(2048), row(128), row(IDX_DIM), row(128)),
        compiler_params=_cparams(("parallel",), VMEM_LIMIT_MB),
        name="odd_post",
    )(p, *tabs128, *tabs64)


def _flash_init(m_ref, l_ref, acc_ref):
    m_ref[...] = jnp.full(m_ref.shape, NEG, F32)
    l_ref[...] = jnp.zeros(l_ref.shape, F32)
    acc_ref[...] = jnp.zeros(acc_ref.shape, F32)


def _flash_step(s, mask, v, m_ref, l_ref, acc_ref):
    if mask is not None:
        s = jnp.where(mask, s, NEG)
    m_prev = m_ref[...]
    m_new = jnp.maximum(m_prev, jnp.max(s, axis=1, keepdims=True))
    alpha = jnp.exp(m_prev - m_new)
    p = jnp.exp(s - m_new)
    if mask is not None:
        p = jnp.where(mask, p, 0.0)
    l_ref[...] = alpha * l_ref[...] + jnp.sum(p, axis=1, keepdims=True)
    acc_ref[...] = alpha * acc_ref[...] + _dot(p.astype(BF16), v)
    m_ref[...] = m_new


def _local_causal(strict):
    r = lax.broadcasted_iota(I32, (QT, QT), 0)
    c = lax.broadcasted_iota(I32, (QT, QT), 1)
    return jnp.where((c < r) if strict else (c <= r), 1.0, 0.0)


def _diag_mask(on_diag, local):
    return jnp.where(on_diag, local, 1.0) > 0.5


def _moba_prompt_body(q_ref, kv_ref, o_ref, m_ref, l_ref, acc_ref, *, nkb):
    i = pl.program_id(1)
    q = q_ref[...]
    qb = q.astype(BF16)
    kmean = jnp.concatenate(
        [jnp.mean(kv_ref[n * QT:(n + 1) * QT, 0:128], axis=0, keepdims=True) for n in range(nkb)], axis=0)
    gate = _dot3_nt(q, kmean)
    blk = lax.broadcasted_iota(I32, gate.shape, 1)
    sel = _topk_rows(jnp.where(blk < i, gate, -jnp.inf), MOBA_TOPK)
    _flash_init(m_ref, l_ref, acc_ref)
    causal = _local_causal(False)
    for n in range(nkb):
        @pl.when(n <= i)
        def _():
            k = kv_ref[n * QT:(n + 1) * QT, 0:128].astype(BF16)
            v = kv_ref[n * QT:(n + 1) * QT, 128:256].astype(BF16)
            s = _dot_nt(qb, k) * ATTN_SCALE
            picked = jnp.broadcast_to(sel[:, n:n + 1], (QT, QT))
            mask = jnp.where(i == n, causal, picked) > 0.5
            _flash_step(s, mask, v, m_ref, l_ref, acc_ref)
    o_ref[...] = (acc_ref[...] / l_ref[...]).astype(o_ref.dtype)


def _mla_prompt_body(q_ref, kv_ref, o_ref, m_ref, l_ref, acc_ref, *, nkb):
    i = pl.program_id(1)
    q = q_ref[...]
    _flash_init(m_ref, l_ref, acc_ref)
    causal = _local_causal(False)
    for n in range(nkb):
        @pl.when(n <= i)
        def _():
            kv = kv_ref[n * QT:(n + 1) * QT, :]
            s = _dot_nt(q, kv) * MLA_SCALE
            _flash_step(s, _diag_mask(i == n, causal), kv[:, 0:MLA_KV_LORA], m_ref, l_ref, acc_ref)
    o_ref[...] = (acc_ref[...] / l_ref[...]).astype(o_ref.dtype)


def _sb_prompt_body(q_ref, kv_ref, tri_ref, o_ref, carry_ref, acc_ref, *, nkb):
    i = pl.program_id(1)
    qb = q_ref[...].astype(BF16)
    carry_ref[...] = jnp.zeros(carry_ref.shape, F32)
    acc_ref[...] = jnp.zeros(acc_ref.shape, F32)
    strict_local = _local_causal(True)
    tri = tri_ref[...]
    for n in reversed(range(nkb)):
        @pl.when(n <= i)
        def _():
            k = kv_ref[n * QT:(n + 1) * QT, 0:128].astype(BF16)
            v = kv_ref[n * QT:(n + 1) * QT, 128:256].astype(BF16)
            z = _dot_nt(qb, k) * ATTN_SCALE
            strict = _diag_mask(i == n, strict_local)
            log_keep = jnp.where(strict, -_softplus(z), 0.0)
            hi, lo = _split_bf16(log_keep)
            later = (_dot(hi, tri) + _dot(lo, tri)) + carry_ref[...]
            a = jnp.where(strict, jnp.exp((z + log_keep) + later), 0.0)
            acc_ref[...] += _dot(a.astype(BF16), v)
            carry_ref[...] += jnp.sum(log_keep, axis=1, keepdims=True)
    o_ref[...] = acc_ref[...].astype(o_ref.dtype)


def _dsa_prompt_body(q_ref, kv_ref, bias_ref, o_ref, m_ref, l_ref, acc_ref, *, nkb):
    i = pl.program_id(1)
    qb = q_ref[...].astype(BF16)
    _flash_init(m_ref, l_ref, acc_ref)
    for n in range(nkb):
        @pl.when(n <= i)
        def _():
            k = kv_ref[n * QT:(n + 1) * QT, 0:128].astype(BF16)
            v = kv_ref[n * QT:(n + 1) * QT, 128:256].astype(BF16)
            mask = bias_ref[:, n * QT:(n + 1) * QT] > 0.5
            s = _dot_nt(qb, k) * ATTN_SCALE
            _flash_step(s, mask, v, m_ref, l_ref, acc_ref)
    o_ref[...] = (acc_ref[...] / l_ref[...]).astype(o_ref.dtype)


def prompt_attention(kind, q, kv, n_batch, seq, extra=None):
    nqt = seq // QT
    nkb = seq // QT
    dq = q.shape[1] // N_HEADS
    dkv = kv.shape[1]
    q_spec = pl.BlockSpec((QT, dq), lambda b, i, h: (b * nqt + i, h))
    kv_spec = pl.BlockSpec((seq, dkv), lambda b, i, h: (b, 0))
    stat = pltpu.VMEM((QT, 1), F32)
    if kind == "moba":
        body, dv, ins, specs = _moba_prompt_body, 128, (q, kv), [q_spec, kv_spec]
        scratch = [stat, stat, pltpu.VMEM((QT, dv), F32)]
    elif kind == "mla":
        body, dv, ins, specs = _mla_prompt_body, MLA_KV_LORA, (q, kv), [q_spec, kv_spec]
        scratch = [stat, stat, pltpu.VMEM((QT, dv), F32)]
    elif kind == "sb":
        body, dv, ins = _sb_prompt_body, 128, (q, kv, extra)
        specs = [q_spec, kv_spec, pl.BlockSpec((QT, QT), lambda b, i, h: (0, 0))]
        scratch = [stat, pltpu.VMEM((QT, dv), F32)]
    else:
        body, dv, ins = _dsa_prompt_body, 128, (q, kv, extra)
        specs = [q_spec, kv_spec, pl.BlockSpec((QT, seq), lambda b, i, h: (b * nqt + i, 0))]
        scratch = [stat, stat, pltpu.VMEM((QT, dv), F32)]
    return pl.pallas_call(
        functools.partial(body, nkb=nkb),
        out_shape=jax.ShapeDtypeStruct((n_batch * seq, N_HEADS * dv), BF16),
        grid=(n_batch, nqt, N_HEADS),
        in_specs=specs,
        out_specs=pl.BlockSpec((QT, dv), lambda b, i, h: (b * nqt + i, h)),
        scratch_shapes=scratch,
        compiler_params=_cparams(("parallel", "parallel", "arbitrary"), VMEM_LIMIT_MB),
        name=kind + "_prompt",
    )(*ins)


def _idx_prompt_body(iq_ref, ikd_ref, iw_ref, o_ref):
    i = pl.program_id(1)
    n = pl.program_id(2)

    @pl.when(n <= i)
    def _():
        ikd = ikd_ref[...]
        lane = lax.broadcasted_iota(I32, (QT, 128), 1)
        acc = jnp.zeros((QT, QT), F32)
        for j in range(IDX_HEADS // 2):
            qp = iq_ref[:, j * 128:(j + 1) * 128]
            zero = jnp.zeros_like(qp)
            le = _dot_nt(jnp.where(lane < IDX_DIM, qp, zero), ikd)
            lo = _dot_nt(jnp.where(lane >= IDX_DIM, qp, zero), ikd)
            acc = acc + iw_ref[:, 2 * j:2 * j + 1] * jnp.maximum(le, 0.0)
            acc = acc + iw_ref[:, 2 * j + 1:2 * j + 2] * jnp.maximum(lo, 0.0)
        o_ref[...] = jnp.where(_diag_mask(i == n, _local_causal(False)), acc, -jnp.inf)

    @pl.when(n > i)
    def _():
        o_ref[...] = jnp.full(o_ref.shape, -jnp.inf, F32)


def idx_prompt_scores(iq, ikd, iw, n_batch, seq):
    nqt = seq // QT
    return pl.pallas_call(
        _idx_prompt_body,
        out_shape=jax.ShapeDtypeStruct((n_batch * seq, seq), F32),
        grid=(n_batch, nqt, nqt),
        in_specs=[pl.BlockSpec((QT, 2048), lambda b, i, n: (b * nqt + i, 0)),
                  pl.BlockSpec((QT, 128), lambda b, i, n: (b * nqt + jnp.minimum(n, i), 0)),
                  pl.BlockSpec((QT, 128), lambda b, i, n: (b * nqt + i, 0))],
        out_specs=pl.BlockSpec((QT, QT), lambda b, i, n: (b * nqt + i, n)),
        compiler_params=_cparams(("parallel", "parallel", "arbitrary"), VMEM_LIMIT_MB),
        name="idx_prompt_scores",
    )(iq, ikd, iw)


def _topk_mask_body(s_ref, pos_ref, o_ref, *, k, idx_bits):
    s = s_ref[...]
    bits = lax.bitcast_convert_type(s, I32)
    key = jnp.where(bits >= 0, bits, bits ^ jnp.int32(0x7FFFFFFF))
    col = lax.broadcasted_iota(I32, s.shape, 1)
    kf = jnp.float32(k)

    def count(mask):
        return jnp.sum(jnp.where(mask, 1.0, 0.0), axis=1, keepdims=True)

    t0 = jnp.where(count(key >= 0) >= kf, jnp.int32(0), jnp.int32(INT_MIN))

    def value_step(b, t):
        cand = t | lax.shift_left(jnp.int32(1), 30 - b)
        return jnp.where(count(key >= cand) >= kf, cand, t)

    thr = lax.fori_loop(0, 31, value_step, t0)
    gt = key > thr
    eq = key == thr
    need = kf - count(gt)
    n_eq = count(eq)

    def tie_step(b, m):
        cand = m | lax.shift_left(jnp.int32(1), idx_bits - 1 - b)
        c = count(jnp.logical_and(eq, col < cand))
        return jnp.where(c < need, cand, m)

    def tie_search():
        return lax.fori_loop(0, idx_bits, tie_step, jnp.zeros(thr.shape, I32))

    def tie_all():
        return jnp.full(thr.shape, 2 ** idx_bits - 1, I32)

    last = lax.cond(jnp.max(n_eq - need) > 0.5, tie_search, tie_all)
    sel = jnp.logical_or(gt, jnp.logical_and(eq, col <= last))
    sel = jnp.logical_and(sel, col <= pos_ref[...])
    o_ref[...] = jnp.where(sel, 1.0, 0.0)


def topk_mask(scores, pos, k, tr):
    r, l = scores.shape
    idx_bits = max(1, (l - 1).bit_length())
    return pl.pallas_call(
        functools.partial(_topk_mask_body, k=k, idx_bits=idx_bits),
        out_shape=jax.ShapeDtypeStruct((r, l), F32),
        grid=(r // tr,),
        in_specs=[pl.BlockSpec((tr, l), lambda i: (i, 0)),
                  pl.BlockSpec((tr, 1), lambda i: (i, 0))],
        out_specs=pl.BlockSpec((tr, l), lambda i: (i, 0)),
        compiler_params=_cparams(("parallel",), VMEM_LIMIT_MB),
        name="topk_mask",
    )(scores, pos)


def _paged_kernel(pt_ref, cache_ref, *refs, body, n_in, n_out, layer, n_pages):
    ins = refs[:n_in]
    outs = refs[n_in:n_in + n_out]
    buf, sem = refs[n_in + n_out:]
    r = pl.program_id(0)
    nr = pl.num_programs(0)
    page_rows = cache_ref.shape[2]

    def page_copy(page, p, slot):
        return pltpu.make_async_copy(cache_ref.at[layer, page],
                                     buf.at[slot, pl.ds(p * page_rows, page_rows)],
                                     sem.at[slot])

    def start_request(req, slot):
        for p in range(n_pages):
            page_copy(pt_ref[req * n_pages + p], p, slot).start()

    @pl.when(r == 0)
    def _():
        start_request(0, 0)

    @pl.when(r + 1 < nr)
    def _():
        start_request(r + 1, (r + 1) % 2)

    slot = r % 2
    for p in range(n_pages):
        page_copy(0, p, slot).wait()
    body(buf.at[slot], *ins, *outs)


def paged_call(name, body, cache, layer, page_table, ins, in_blocks, out_shape, out_block):
    n_req, n_pages = page_table.shape
    page_rows, w = cache.shape[2:]
    nd = lambda blk: (lambda r, pt: (r,) + (0,) * (len(blk) - 1))
    const = lambda blk: (lambda r, pt: (0,) * len(blk))
    in_specs = [pl.BlockSpec(memory_space=pl.ANY)]
    for blk, per_req in in_blocks:
        in_specs.append(pl.BlockSpec(blk, nd(blk) if per_req else const(blk)))
    grid_spec = pltpu.PrefetchScalarGridSpec(
        num_scalar_prefetch=1,
        grid=(n_req,),
        in_specs=in_specs,
        out_specs=pl.BlockSpec(out_block, nd(out_block)),
        scratch_shapes=[pltpu.VMEM((2, n_pages * page_rows, w), F32),
                        pltpu.SemaphoreType.DMA((2,))])
    return pl.pallas_call(
        functools.partial(_paged_kernel, body=body, n_in=len(ins), n_out=1, layer=layer, n_pages=n_pages),
        out_shape=out_shape,
        grid_spec=grid_spec,
        compiler_params=_cparams(("arbitrary",), VMEM_LIMIT_MB),
        name=name,
    )(page_table.reshape(-1), cache, *ins)


def _softmax_with_new(s, mask, s_new, v, v_new):
    if mask is not None:
        s = jnp.where(mask, s, NEG)
    m = jnp.maximum(jnp.max(s, axis=1, keepdims=True), s_new)
    p = jnp.exp(s - m)
    if mask is not None:
        p = jnp.where(mask, p, 0.0)
    p_new = jnp.exp(s_new - m)
    l = jnp.sum(p, axis=1, keepdims=True) + p_new
    acc = _dot(p.astype(BF16), v) + p_new.astype(BF16).astype(F32) * v_new.astype(BF16).astype(F32)
    return acc / l


def _score_new(q, k_new):
    return jnp.sum(q.astype(BF16).astype(F32) * k_new.astype(BF16).astype(F32), axis=1, keepdims=True)


def _kv_rows(win):
    n = win.shape[0] // 2
    return win[pl.ds(0, n, stride=2), :], win[pl.ds(1, n, stride=2), :]


def _moba_decode_body(win, q_ref, new_ref, e_ref, o_ref):
    q = q_ref[0]
    k, v = _kv_rows(win)
    v = v.astype(BF16)
    nb = k.shape[0] // MOBA_BLOCK
    kmean = jnp.mean(k.reshape(nb, MOBA_BLOCK, 128), axis=1)
    sel = _topk_rows(_dot3_nt(q, kmean), MOBA_TOPK)
    picked = _dot(sel.astype(BF16), e_ref[...]) > 0.5
    s = _dot_nt(q.astype(BF16), k.astype(BF16)) * ATTN_SCALE
    new = new_ref[0]
    s_new = _score_new(q, new[:, 0:128]) * ATTN_SCALE
    o_ref[0] = _softmax_with_new(s, picked, s_new, v, new[:, 128:256])


def _mla_decode_body(win, q_ref, new_ref, o_ref):
    q = q_ref[0][:, 0:MLA_KV_LORA + MLA_ROPE]
    kv = win[...].astype(BF16)
    s = _dot_nt(q, kv) * MLA_SCALE
    new = new_ref[0]
    s_new = _score_new(q.astype(F32), new) * MLA_SCALE
    o_ref[0] = _softmax_with_new(s, None, s_new, kv[:, 0:MLA_KV_LORA], new[:, 0:MLA_KV_LORA]).astype(o_ref.dtype)


def _sb_decode_body(win, q_ref, o_ref):
    q = q_ref[0].astype(BF16)
    k, v = _kv_rows(win)
    k = k.astype(BF16)
    v = v.astype(BF16)
    z = _dot_nt(q, k) * ATTN_SCALE
    log_keep = -_softplus(z)
    n = z.shape[1]
    col = lax.broadcasted_iota(I32, z.shape, 1)
    suffix = log_keep
    d = 1
    while d < n:
        suffix = suffix + jnp.where(col + d < n, pltpu.roll(suffix, n - d, 1), 0.0)
        d *= 2
    a = jnp.exp((z + log_keep) + (suffix - log_keep))
    o_ref[0] = _dot(a.astype(BF16), v)


def _idx_decode_body(win, iq_ref, iw_ref, new_ref, o_ref):
    iq = iq_ref[0]
    w = iw_ref[0]
    logits = _dot_nt(iq, win[...].astype(BF16))
    score = jnp.sum(w * jnp.maximum(logits, 0.0), axis=0, keepdims=True)
    l_new = _score_new(iq.astype(F32), new_ref[0])
    s_new = jnp.sum(w * jnp.maximum(l_new, 0.0), axis=0, keepdims=True)
    n = score.shape[1]
    o_ref[0, :, 0:n] = score
    lane = lax.broadcasted_iota(I32, (1, 128), 1)
    o_ref[0, :, n:n + 128] = jnp.where(lane == 0, s_new, -jnp.inf)


def _dsa_decode_body(win, q_ref, new_ref, mask_ref, o_ref):
    q = q_ref[0]
    k, v = _kv_rows(win)
    k = k.astype(BF16)
    v = v.astype(BF16)
    n = k.shape[0]
    s = _dot_nt(q.astype(BF16), k) * ATTN_SCALE
    picked = jnp.broadcast_to(mask_ref[0, :, 0:n], s.shape) > 0.5
    new = new_ref[0]
    new_on = mask_ref[0, :, n:n + 1] > 0.5
    s_new = jnp.where(new_on, _score_new(q, new[:, 0:128]) * ATTN_SCALE, NEG)
    o_ref[0] = _softmax_with_new(s, picked, s_new, v, new[:, 128:256])


def _rope_tables(pos):
    posf = pos.astype(F32)[:, None]
    inv128 = ROPE_THETA ** (-2.0 * jnp.arange(64, dtype=F32) / 128)
    ang = posf * inv128[None, :]
    c, s = jnp.cos(ang), jnp.sin(ang)
    tabs128 = (jnp.concatenate([c, c], axis=1), jnp.concatenate([-s, s], axis=1))
    inv64 = ROPE_THETA ** (-2.0 * jnp.arange(32, dtype=F32) / 64)
    ang = posf * inv64[None, :]
    c, s = jnp.cos(ang), jnp.sin(ang)
    z = jnp.zeros_like(s)
    tabs64 = (jnp.concatenate([c, c, c, c], axis=1),
              jnp.concatenate([-s, z, -s, z], axis=1),
              jnp.concatenate([z, s, z, s], axis=1))
    return tabs128, tabs64


def _pad_cols(w, n):
    return jnp.pad(w, ((0, 0), (0, n - w.shape[1])))


def _even_w_in(w):
    return _pad_cols(w, EVEN_PAD).astype(BF16)


def _odd_w_in(w):
    return jnp.concatenate([w[:, 0:6656], _pad_cols(w[:, 6656:6688], 128), _pad_cols(w[:, 6688:6752], 128)],
                           axis=1).astype(BF16)


def _mla_w_uq(w):
    w3 = w.reshape(w.shape[0], N_HEADS, MLA_QK)
    nope = w3[:, :, 0:128].reshape(w.shape[0], N_HEADS * 128)
    rope = jnp.pad(w3[:, :, 128:], ((0, 0), (0, 0), (0, 64))).reshape(w.shape[0], N_HEADS * 128)
    return jnp.concatenate([nope, rope], axis=1).astype(BF16)


def _block_membership(n_keys):
    blk = jnp.arange(n_keys, dtype=I32)[None, :] // MOBA_BLOCK
    return (blk == jnp.arange(n_keys // MOBA_BLOCK, dtype=I32)[:, None]).astype(BF16)


def _kv_row_view(cache):
    return cache.reshape(cache.shape[0], cache.shape[1], 2 * cache.shape[2], cache.shape[4])


def _even_mixer(h, x, j, dims, tabs, caches, page_table, w):
    n_batch, seq, n_req, tm, tr = dims
    n_prompt = n_batch * seq
    tabs128, tabs64 = tabs
    cache_moba, cache_mla = caches
    p = matmul(h, _even_w_in(w["even_w_in"][j]), F32, tm, EVEN_PAD // 9)
    mq, moba_new, cqn, mla_new, kv_pad = even_post(p, tabs128, tabs64, w["mla_q_norm_w"][j],
                                                   w["mla_kv_norm_w"][j], tr)
    qfull = matmul(cqn, _mla_w_uq(w["mla_w_uq"][j]), F32, tm, 512)
    w_uk_t = jnp.transpose(w["mla_w_uk"][j], (1, 2, 0)).astype(BF16)
    w_uv = jnp.transpose(w["mla_w_uv"][j], (1, 0, 2)).astype(BF16)
    q_cat = mla_qcat(qfull, w_uk_t, tabs64, tm)

    moba_p = prompt_attention("moba", mq, moba_new, n_batch, seq)
    mla_p = prompt_attention("mla", q_cat, kv_pad, n_batch, seq)

    n_past = page_table.shape[1] * PAGE_SIZE
    moba_s = paged_call(
        "moba_decode", _moba_decode_body, _kv_row_view(cache_moba), j, page_table,
        (mq[n_prompt:].reshape(n_req, N_HEADS, 128), moba_new[n_prompt:].reshape(n_req, 1, 256),
         _block_membership(n_past)),
        [((1, N_HEADS, 128), True), ((1, 1, 256), True), ((n_past // MOBA_BLOCK, n_past), False)],
        jax.ShapeDtypeStruct((n_req, N_HEADS, 128), F32), (1, N_HEADS, 128))
    mla_s = paged_call(
        "mla_decode", _mla_decode_body, cache_mla, j, page_table,
        (q_cat[n_prompt:].reshape(n_req, N_HEADS, MLA_KV_PAD), mla_new[n_prompt:].reshape(n_req, 1, -1)),
        [((1, N_HEADS, MLA_KV_PAD), True), ((1, 1, MLA_KV_LORA + MLA_ROPE), True)],
        jax.ShapeDtypeStruct((n_req, N_HEADS, MLA_KV_LORA), BF16), (1, N_HEADS, MLA_KV_LORA))

    moba_out = jnp.concatenate([moba_p, moba_s.reshape(n_req, -1).astype(BF16)], axis=0)
    mla_lat = jnp.concatenate([mla_p, mla_s.reshape(n_req, -1)], axis=0)
    mla_out = headwise_matmul(mla_lat, w_uv, BF16, tm)
    x = matmul2_residual(moba_out, mla_out, w["even_w_out"], (j,), x, tm, 256)
    return x, (moba_new, mla_new)


def _odd_mixer(h, x, j, dims, tabs, caches, page_table, w, consts):
    n_batch, seq, n_req, tm, tr = dims
    n_prompt = n_batch * seq
    tabs128, tabs64 = tabs
    cache_sb, cache_dsa, cache_idx = caches
    tri, pos_prompt, pos_sample = consts
    p = matmul(h, _odd_w_in(w["odd_w_in"][j]), F32, tm, ODD_PAD // 9)
    sq, sb_new, dq, dsa_new, iq, iw, idx_new, ikd = odd_post(p, tabs128, tabs64, tr)

    sb_p = prompt_attention("sb", sq, sb_new, n_batch, seq, tri)
    scores_p = idx_prompt_scores(iq, ikd, iw, n_batch, seq)
    mask_p = topk_mask(scores_p, pos_prompt, min(DSA_TOPK, seq // 4), QT)
    dsa_p = prompt_attention("dsa", dq, dsa_new, n_batch, seq, mask_p)

    n_past = page_table.shape[1] * PAGE_SIZE
    sb_s = paged_call(
        "sb_decode", _sb_decode_body, _kv_row_view(cache_sb), j, page_table,
        (sq[n_prompt:].reshape(n_req, N_HEADS, 128),),
        [((1, N_HEADS, 128), True)],
        jax.ShapeDtypeStruct((n_req, N_HEADS, 128), F32), (1, N_HEADS, 128))
    scores_s = paged_call(
        "idx_decode", _idx_decode_body, cache_idx, j, page_table,
        (iq[n_prompt:].reshape(n_req, IDX_HEADS, IDX_DIM), iw[n_prompt:, 0:IDX_HEADS].reshape(n_req, IDX_HEADS, 1),
         idx_new[n_prompt:].reshape(n_req, 1, IDX_DIM)),
        [((1, IDX_HEADS, IDX_DIM), True), ((1, IDX_HEADS, 1), True), ((1, 1, IDX_DIM), True)],
        jax.ShapeDtypeStruct((n_req, 1, n_past + 128), F32), (1, 1, n_past + 128))
    mask_s = topk_mask(scores_s.reshape(n_req, n_past + 128), pos_sample, min(DSA_TOPK, (n_past + 1) // 4), n_req)
    dsa_s = paged_call(
        "dsa_decode", _dsa_decode_body, _kv_row_view(cache_dsa), j, page_table,
        (dq[n_prompt:].reshape(n_req, N_HEADS, 128), dsa_new[n_prompt:].reshape(n_req, 1, 256),
         mask_s.reshape(n_req, 1, n_past + 128)),
        [((1, N_HEADS, 128), True), ((1, 1, 256), True), ((1, 1, n_past + 128), True)],
        jax.ShapeDtypeStruct((n_req, N_HEADS, 128), F32), (1, N_HEADS, 128))

    sb_out = jnp.concatenate([sb_p, sb_s.reshape(n_req, -1).astype(BF16)], axis=0)
    dsa_out = jnp.concatenate([dsa_p, dsa_s.reshape(n_req, -1).astype(BF16)], axis=0)
    x = matmul2_residual(sb_out, dsa_out, w["odd_w_out"], (j,), x, tm, 256)
    return x, (sb_new, dsa_new, idx_new)


def _trunk(x_prompt, x_sample, caches, page_table, w, tm, tr):
    n_batch, seq, d = x_prompt.shape
    n_req = x_sample.shape[0]
    n_prompt = n_batch * seq
    n_past = page_table.shape[1] * PAGE_SIZE
    depth = w["norm_w"].shape[0]
    dims = (n_batch, seq, n_req, tm, tr)
    x = jnp.concatenate([x_prompt.reshape(n_prompt, d), x_sample.reshape(n_req, d)], axis=0)
    pos = jnp.concatenate([jnp.tile(jnp.arange(seq, dtype=I32), n_batch), jnp.full((n_req,), n_past, I32)])
    tabs = _rope_tables(pos)
    tri = (jnp.arange(QT)[:, None] > jnp.arange(QT)[None, :]).astype(BF16)
    consts = (tri, pos[:n_prompt, None], pos[n_prompt:, None])
    cache_moba, cache_mla, cache_sb, cache_dsa, cache_idx = caches
    states = []
    tn = 256
    for layer in range(depth):
        j = layer // 2
        nw = w["norm_w"][layer]
        h = rmsnorm(x, nw[0], BF16, tr)
        a = ffn_up(h, w["ffn_w_gate"], w["ffn_w_up"], (layer, 0), tm, tn)
        x = matmul_residual(a, w["ffn_w_down"], (layer, 0), x, 0.5, tm, tn)
        h = rmsnorm(x, nw[1], BF16, tr)
        if layer % 2 == 0:
            x, st = _even_mixer(h, x, j, dims, tabs, (cache_moba, cache_mla), page_table, w)
        else:
            x, st = _odd_mixer(h, x, j, dims, tabs, (cache_sb, cache_dsa, cache_idx), page_table, w, consts)
        states.append(st)
        h = rmsnorm(x, nw[2], BF16, tr)
        a = ffn_up(h, w["ffn_w_gate"], w["ffn_w_up"], (layer, 1), tm, tn)
        x = matmul_residual(a, w["ffn_w_down"], (layer, 1), x, 0.5, tm, tn)
    y = rmsnorm(x, w["final_norm_w"], F32, tr)

    def split(rows, tail):
        a = jnp.stack(rows)
        return (a[:, :n_prompt].reshape(a.shape[0], n_batch, seq, *tail),
                a[:, n_prompt:].reshape(a.shape[0], n_req, 1, *tail))

    even = states[0::2]
    odd = states[1::2]
    moba_p, moba_s = split([s[0] for s in even], (2, HEAD_DIM))
    mla_p, mla_s = split([s[1] for s in even], (MLA_KV_LORA + MLA_ROPE,))
    sb_p, sb_s = split([s[0] for s in odd], (2, HEAD_DIM))
    dsa_p, dsa_s = split([s[1] for s in odd], (2, HEAD_DIM))
    idx_p, idx_s = split([s[2] for s in odd], (IDX_DIM,))
    return (y[:n_prompt].reshape(n_batch, seq, d), y[n_prompt:].reshape(n_req, 1, d),
            moba_p, moba_s, mla_p, mla_s, sb_p, sb_s, dsa_p, dsa_s, idx_p, idx_s)


def kernel(x_prompt, x_sample, cache_moba_kv, cache_mla_latent, cache_sb_kv, cache_dsa_kv, cache_dsa_idx_k, page_table, norm_w, ffn_w_gate, ffn_w_up, ffn_w_down, even_w_in, even_w_out, mla_q_norm_w, mla_w_uq, mla_kv_norm_w, mla_w_uk, mla_w_uv, odd_w_in, odd_w_out, final_norm_w):
    w = dict(norm_w=norm_w, ffn_w_gate=ffn_w_gate, ffn_w_up=ffn_w_up, ffn_w_down=ffn_w_down,
             even_w_in=even_w_in, even_w_out=even_w_out, mla_q_norm_w=mla_q_norm_w, mla_w_uq=mla_w_uq,
             mla_kv_norm_w=mla_kv_norm_w, mla_w_uk=mla_w_uk, mla_w_uv=mla_w_uv,
             odd_w_in=odd_w_in, odd_w_out=odd_w_out, final_norm_w=final_norm_w)
    caches = (cache_moba_kv, cache_mla_latent, cache_sb_kv, cache_dsa_kv, cache_dsa_idx_k)
    n_tokens = x_prompt.shape[0] * x_prompt.shape[1] + x_sample.shape[0]
    return _trunk(x_prompt, x_sample, caches, page_table, w, n_tokens // 8, n_tokens // 40)
```

```python
import functools

import jax
import jax.numpy as jnp
from jax import lax
from jax.experimental import pallas as pl
from jax.experimental.pallas import tpu as pltpu

F32 = jnp.float32
BF16 = jnp.bfloat16
I32 = jnp.int32

D_MODEL = 4096
HEAD_DIM = 128
N_HEADS = 16
MOBA_BLOCK = 256
MOBA_TOPK = 3
MLA_Q_LORA = 768
MLA_KV_LORA = 256
MLA_ROPE = 64
MLA_QK = 192
IDX_HEADS = 32
IDX_DIM = 64
DSA_TOPK = 256
PAGE_SIZE = 128
ROPE_THETA = 10000.0
EPS = 1e-6
ATTN_SCALE = HEAD_DIM ** -0.5
MLA_SCALE = MLA_QK ** -0.5
IDX_SCALE = (IDX_HEADS * IDX_DIM) ** -0.5
NEG = -1e30
INT_MIN = -2 ** 31

EVEN_PAD = 3584
ODD_PAD = 6912
MLA_KV_PAD = 384
QT = 256
HEADS_PER_STEP = 16
VMEM_LIMIT_MB = 52


def _cparams(sem, vmem_mb=None):
    return pltpu.CompilerParams(
        dimension_semantics=sem,
        vmem_limit_bytes=None if vmem_mb is None else vmem_mb * 2 ** 20)


def _dot(a, b):
    return jnp.dot(a, b, preferred_element_type=F32)


def _dot_nt(a, b):
    return lax.dot_general(a, b, (((1,), (1,)), ((), ())), preferred_element_type=F32)


def _split_bf16(x):
    hi = x.astype(BF16)
    lo = (x - hi.astype(F32)).astype(BF16)
    return hi, lo


def _dot3_nt(a, b):
    ah, al = _split_bf16(a)
    bh, bl = _split_bf16(b)
    return _dot_nt(ah, bh) + (_dot_nt(ah, bl) + _dot_nt(al, bh))


def _softplus(z):
    return jnp.maximum(z, 0.0) + jnp.log(1.0 + jnp.exp(-jnp.abs(z)))


def _rope128(x, cos, sin_signed):
    return x * cos + pltpu.roll(x, 64, 1) * sin_signed


def _rope64(x, cos, sin_a, sin_b):
    return x * cos + pltpu.roll(x, 96, 1) * sin_a + pltpu.roll(x, 32, 1) * sin_b


def _topk_rows(g, k, axis=1):
    col = lax.broadcasted_iota(I32, g.shape, axis).astype(F32)
    big = jnp.float32(g.shape[axis])
    sel = jnp.zeros(g.shape, F32)
    for _ in range(k):
        mx = jnp.max(g, axis=axis, keepdims=True)
        cand = jnp.where(g == mx, col, big)
        cand = jnp.where(mx > -jnp.inf, cand, big)
        first = jnp.min(cand, axis=axis, keepdims=True)
        pick = col == first
        sel = jnp.where(pick, 1.0, sel)
        g = jnp.where(pick, -jnp.inf, g)
    return sel


def _rmsnorm_body(x_ref, w_ref, o_ref):
    x = x_ref[...]
    ms = jnp.mean(x * x, axis=-1, keepdims=True)
    o_ref[...] = ((x * lax.rsqrt(ms + EPS)) * w_ref[...]).astype(o_ref.dtype)


def rmsnorm(x, w, out_dtype, tm):
    m, d = x.shape
    return pl.pallas_call(
        _rmsnorm_body,
        out_shape=jax.ShapeDtypeStruct((m, d), out_dtype),
        grid=(m // tm,),
        in_specs=[pl.BlockSpec((tm, d), lambda i: (i, 0)),
                  pl.BlockSpec((1, d), lambda i: (0, 0))],
        out_specs=pl.BlockSpec((tm, d), lambda i: (i, 0)),
        compiler_params=_cparams(("parallel",), VMEM_LIMIT_MB),
        name="rmsnorm",
    )(x, w.reshape(1, d))


def _row_tile_spec(tm, k):
    return pl.BlockSpec((tm, k), lambda i, j: (i, 0), pipeline_mode=pl.Buffered(1))


def _mm_body(a_ref, b_ref, o_ref):
    o_ref[...] = _dot(a_ref[...].astype(BF16), b_ref[...].astype(BF16)).astype(o_ref.dtype)


def matmul(a, b, out_dtype, tm, tn):
    m, k = a.shape
    n = b.shape[1]
    return pl.pallas_call(
        _mm_body,
        out_shape=jax.ShapeDtypeStruct((m, n), out_dtype),
        grid=(m // tm, n // tn),
        in_specs=[_row_tile_spec(tm, k),
                  pl.BlockSpec((k, tn), lambda i, j: (0, j))],
        out_specs=pl.BlockSpec((tm, tn), lambda i, j: (i, j)),
        compiler_params=_cparams(("parallel", "parallel"), VMEM_LIMIT_MB),
        name="matmul",
    )(a, b)


def _weight_spec(w, lead, tn):
    k = w.shape[-2]
    return pl.BlockSpec((None,) * len(lead) + (k, tn), lambda i, j: tuple(lead) + (0, j))


def _ffn_up_body(a_ref, g_ref, u_ref, o_ref):
    a = a_ref[...]
    g = _dot(a, g_ref[...].astype(BF16))
    u = _dot(a, u_ref[...].astype(BF16))
    o_ref[...] = ((g * (1.0 / (1.0 + jnp.exp(-g)))) * u).astype(o_ref.dtype)


def ffn_up(h, w_gate, w_up, lead, tm, tn):
    m, k = h.shape
    n = w_gate.shape[-1]
    return pl.pallas_call(
        _ffn_up_body,
        out_shape=jax.ShapeDtypeStruct((m, n), BF16),
        grid=(m // tm, n // tn),
        in_specs=[_row_tile_spec(tm, k),
                  _weight_spec(w_gate, lead, tn),
                  _weight_spec(w_up, lead, tn)],
        out_specs=pl.BlockSpec((tm, tn), lambda i, j: (i, j)),
        compiler_params=_cparams(("parallel", "parallel"), VMEM_LIMIT_MB),
        name="ffn_up",
    )(h, w_gate, w_up)


def _mm_res_body(a_ref, b_ref, x_ref, o_ref, *, scale):
    acc = _dot(a_ref[...], b_ref[...].astype(BF16))
    o_ref[...] = x_ref[...] + scale * acc


def matmul_residual(a, b, lead, x, scale, tm, tn):
    m, k = a.shape
    n = b.shape[-1]
    return pl.pallas_call(
        functools.partial(_mm_res_body, scale=scale),
        out_shape=jax.ShapeDtypeStruct((m, n), F32),
        grid=(m // tm, n // tn),
        in_specs=[_row_tile_spec(tm, k),
                  _weight_spec(b, lead, tn),
                  pl.BlockSpec((tm, tn), lambda i, j: (i, j))],
        out_specs=pl.BlockSpec((tm, tn), lambda i, j: (i, j)),
        compiler_params=_cparams(("parallel", "parallel"), VMEM_LIMIT_MB),
        name="matmul_residual",
    )(a, b, x)


def _mm2_res_body(a1_ref, a2_ref, b_ref, x_ref, o_ref):
    k1 = a1_ref.shape[1]
    acc = _dot(a1_ref[...], b_ref[:k1, :].astype(BF16)) + _dot(a2_ref[...], b_ref[k1:, :].astype(BF16))
    o_ref[...] = x_ref[...] + acc


def matmul2_residual(a1, a2, b, lead, x, tm, tn):
    m, k1 = a1.shape
    k2 = a2.shape[1]
    n = b.shape[-1]
    return pl.pallas_call(
        _mm2_res_body,
        out_shape=jax.ShapeDtypeStruct((m, n), F32),
        grid=(m // tm, n // tn),
        in_specs=[_row_tile_spec(tm, k1),
                  _row_tile_spec(tm, k2),
                  _weight_spec(b, lead, tn),
                  pl.BlockSpec((tm, tn), lambda i, j: (i, j))],
        out_specs=pl.BlockSpec((tm, tn), lambda i, j: (i, j)),
        compiler_params=_cparams(("parallel", "parallel"), VMEM_LIMIT_MB),
        name="matmul2_residual",
    )(a1, a2, b, x)


def _headwise_body(x_ref, w_ref, o_ref):
    o_ref[...] = _dot(x_ref[...].astype(BF16), w_ref[0]).astype(o_ref.dtype)


def headwise_matmul(x, w, out_dtype, tm):
    m = x.shape[0]
    nh, din, dout = w.shape
    return pl.pallas_call(
        _headwise_body,
        out_shape=jax.ShapeDtypeStruct((m, nh * dout), out_dtype),
        grid=(m // tm, nh),
        in_specs=[pl.BlockSpec((tm, din), lambda i, h: (i, h)),
                  pl.BlockSpec((1, din, dout), lambda i, h: (h, 0, 0))],
        out_specs=pl.BlockSpec((tm, dout), lambda i, h: (i, h)),
        compiler_params=_cparams(("parallel", "parallel"), VMEM_LIMIT_MB),
        name="headwise_matmul",
    )(x, w)


def _mla_qcat_body(qn_ref, qr_ref, w_ref, c_ref, sa_ref, sb_ref, o_ref):
    lat = _dot(qn_ref[...].astype(BF16), w_ref[0])
    rot = _rope64(qr_ref[...], c_ref[...], sa_ref[...], sb_ref[...])
    o_ref[:, :MLA_KV_LORA] = lat.astype(o_ref.dtype)
    o_ref[:, MLA_KV_LORA:] = rot.astype(o_ref.dtype)


def mla_qcat(q, w_uk_t, tabs64, tm):
    m = q.shape[0]
    c64, sa, sb = tabs64
    tab = pl.BlockSpec((tm, 128), lambda i, h: (i, 0))
    return pl.pallas_call(
        _mla_qcat_body,
        out_shape=jax.ShapeDtypeStruct((m, N_HEADS * MLA_KV_PAD), BF16),
        grid=(m // tm, N_HEADS),
        in_specs=[pl.BlockSpec((tm, 128), lambda i, h: (i, h)),
                  pl.BlockSpec((tm, 128), lambda i, h: (i, N_HEADS + h)),
                  pl.BlockSpec((1, 128, MLA_KV_LORA), lambda i, h: (h, 0, 0)),
                  tab, tab, tab],
        out_specs=pl.BlockSpec((tm, MLA_KV_PAD), lambda i, h: (i, h)),
        compiler_params=_cparams(("parallel", "parallel"), VMEM_LIMIT_MB),
        name="mla_qcat",
    )(q, q, w_uk_t, c64, sa, sb)


def _even_post_body(p_ref, cos_ref, sin_ref, c64_ref, sa_ref, sb_ref, qnw_ref, kvnw_ref,
                    mq_ref, moba_ref, moba_bf_ref, cqn_ref, mla_ref, kvp_ref):
    cos = cos_ref[...]
    sin = sin_ref[...]
    for h in range(N_HEADS):
        sl = slice(h * 128, (h + 1) * 128)
        mq_ref[:, sl] = _rope128(p_ref[:, sl], cos, sin)
    moba_ref[:, 0:128] = _rope128(p_ref[:, 2048:2176], cos, sin)
    moba_ref[:, 128:256] = p_ref[:, 2176:2304]
    moba_bf_ref[...] = moba_ref[...].astype(moba_bf_ref.dtype)
    cq = p_ref[:, 2304:3072]
    ms = jnp.mean(cq * cq, axis=-1, keepdims=True)
    cqn_ref[...] = ((cq * lax.rsqrt(ms + EPS)) * qnw_ref[...]).astype(cqn_ref.dtype)
    ckv = p_ref[:, 3072:3328]
    ms = jnp.mean(ckv * ckv, axis=-1, keepdims=True)
    ckvn = (ckv * lax.rsqrt(ms + EPS)) * kvnw_ref[...]
    kr = _rope64(p_ref[:, 3328:3456], c64_ref[...], sa_ref[...], sb_ref[...])
    mla_ref[:, 0:MLA_KV_LORA] = ckvn
    mla_ref[:, MLA_KV_LORA:MLA_KV_LORA + MLA_ROPE] = kr[:, 0:MLA_ROPE]
    kvp_ref[:, 0:MLA_KV_LORA] = ckvn.astype(kvp_ref.dtype)
    kvp_ref[:, MLA_KV_LORA:] = kr.astype(kvp_ref.dtype)


def even_post(p, tabs128, tabs64, q_norm_w, kv_norm_w, tm):
    m = p.shape[0]
    row = lambda w: pl.BlockSpec((tm, w), lambda i: (i, 0))
    return pl.pallas_call(
        _even_post_body,
        out_shape=(jax.ShapeDtypeStruct((m, 2048), F32),
                   jax.ShapeDtypeStruct((m, 256), F32),
                   jax.ShapeDtypeStruct((m, 256), BF16),
                   jax.ShapeDtypeStruct((m, MLA_Q_LORA), BF16),
                   jax.ShapeDtypeStruct((m, MLA_KV_LORA + MLA_ROPE), F32),
                   jax.ShapeDtypeStruct((m, MLA_KV_PAD), BF16)),
        grid=(m // tm,),
        in_specs=[row(EVEN_PAD), row(128), row(128), row(128), row(128), row(128),
                  pl.BlockSpec((1, MLA_Q_LORA), lambda i: (0, 0)),
                  pl.BlockSpec((1, MLA_KV_LORA), lambda i: (0, 0))],
        out_specs=(row(2048), row(256), row(256), row(MLA_Q_LORA), row(MLA_KV_LORA + MLA_ROPE),
                   row(MLA_KV_PAD)),
        compiler_params=_cparams(("parallel",), VMEM_LIMIT_MB),
        name="even_post",
    )(p, *tabs128, *tabs64, q_norm_w.reshape(1, -1), kv_norm_w.reshape(1, -1))


def _odd_post_body(p_ref, cos_ref, sin_ref, c64_ref, sa_ref, sb_ref,
                   sq_ref, sb_new_ref, sb_bf_ref, dq_ref, dsa_new_ref, dsa_bf_ref,
                   iq_ref, iw_ref, idx_new_ref, ikd_ref):
    cos = cos_ref[...]
    sin = sin_ref[...]
    c64 = c64_ref[...]
    sa = sa_ref[...]
    sb = sb_ref[...]
    sq_ref[...] = p_ref[:, 0:2048]
    sb_new_ref[...] = p_ref[:, 2048:2304]
    sb_bf_ref[...] = p_ref[:, 2048:2304].astype(sb_bf_ref.dtype)
    for h in range(N_HEADS):
        sl = slice(h * 128, (h + 1) * 128)
        dq_ref[:, sl] = _rope128(p_ref[:, 2304 + h * 128:2304 + (h + 1) * 128], cos, sin)
        iq_ref[:, sl] = _rope64(p_ref[:, 4608 + h * 128:4608 + (h + 1) * 128], c64, sa, sb).astype(iq_ref.dtype)
    dsa_new_ref[:, 0:128] = _rope128(p_ref[:, 4352:4480], cos, sin)
    dsa_new_ref[:, 128:256] = p_ref[:, 4480:4608]
    dsa_bf_ref[...] = dsa_new_ref[...].astype(dsa_bf_ref.dtype)
    iw_ref[...] = p_ref[:, 6656:6784] * IDX_SCALE
    ik = _rope64(p_ref[:, 6784:6912], c64, sa, sb)
    idx_new_ref[...] = ik[:, 0:IDX_DIM]
    ikd_ref[...] = (ik + pltpu.roll(ik, 64, 1)).astype(ikd_ref.dtype)


def odd_post(p, tabs128, tabs64, tm):
    m = p.shape[0]
    row = lambda w: pl.BlockSpec((tm, w), lambda i: (i, 0))
    return pl.pallas_call(
        _odd_post_body,
        out_shape=(jax.ShapeDtypeStruct((m, 2048), F32),
                   jax.ShapeDtypeStruct((m, 256), F32),
                   jax.ShapeDtypeStruct((m, 256), BF16),
                   jax.ShapeDtypeStruct((m, 2048), F32),
                   jax.ShapeDtypeStruct((m, 256), F32),
                   jax.ShapeDtypeStruct((m, 256), BF16),
                   jax.ShapeDtypeStruct((m, 2048), BF16),
                   jax.ShapeDtypeStruct((m, 128), F32),
                   jax.ShapeDtypeStruct((m, IDX_DIM), F32),
                   jax.ShapeDtypeStruct((m, 128), BF16)),
        grid=(m // tm,),
        in_specs=[row(ODD_PAD), row(128), row(128), row(128), row(128), row(128)],
        out_specs=(row(2048), row(256), row(256), row(2048), row(256), row(256), row(2048), row(128),
                   row(IDX_DIM), row(128)),
        compiler_params=_cparams(("parallel",), VMEM_LIMIT_MB),
        name="odd_post",
    )(p, *tabs128, *tabs64)


def _flash_init(m_ref, l_ref, acc_ref):
    m_ref[...] = jnp.full(m_ref.shape, NEG, F32)
    l_ref[...] = jnp.zeros(l_ref.shape, F32)
    acc_ref[...] = jnp.zeros(acc_ref.shape, F32)


def _stack_heads(q_ref, d):
    return jnp.concatenate([q_ref[:, h * d:(h + 1) * d] for h in range(q_ref.shape[1] // d)], axis=0)


def _unstack_heads(o_ref, x):
    dv = x.shape[1]
    for h in range(x.shape[0] // QT):
        o_ref[:, h * dv:(h + 1) * dv] = x[h * QT:(h + 1) * QT].astype(o_ref.dtype)


def _lanes(x, n):
    return x if n == x.shape[1] else jnp.concatenate([x] * (n // x.shape[1]), axis=1)


def _rows(x, n):
    return x if n == x.shape[0] else jnp.concatenate([x] * (n // x.shape[0]), axis=0)


def _row_stat(x):
    return jnp.broadcast_to(x, (x.shape[0], 128))


def _key_block(ref, n, cols=slice(None)):
    return ref[pl.ds(pl.multiple_of(n * QT, QT), QT), cols]


def _flash_block(q, k, v, bias, scale, m_ref, l_ref, acc_ref):
    s = _dot_nt(q, k) * scale
    if bias is not None:
        s = s + bias
    m_prev = m_ref[...]
    m_new = jnp.maximum(m_prev, _row_stat(jnp.max(s, axis=1, keepdims=True)))
    alpha = jnp.exp(m_prev - m_new)
    p = jnp.exp(s - _lanes(m_new, s.shape[1]))
    l_ref[...] = alpha * l_ref[...] + _row_stat(jnp.sum(p, axis=1, keepdims=True))
    acc_ref[...] = _lanes(alpha, acc_ref.shape[1]) * acc_ref[...] + _dot(p.astype(BF16), v)
    m_ref[...] = m_new


def _flash_finish(o_ref, l_ref, acc_ref):
    _unstack_heads(o_ref, acc_ref[...] / _lanes(l_ref[...], acc_ref.shape[1]))


def _causal_bias():
    r = lax.broadcasted_iota(I32, (QT, QT), 0)
    c = lax.broadcasted_iota(I32, (QT, QT), 1)
    return jnp.where(c <= r, 0.0, NEG)


def _local_causal(strict):
    r = lax.broadcasted_iota(I32, (QT, QT), 0)
    c = lax.broadcasted_iota(I32, (QT, QT), 1)
    return jnp.where((c < r) if strict else (c <= r), 1.0, 0.0)


def _diag_mask(on_diag, local):
    return jnp.where(on_diag, local, 1.0) > 0.5


def _moba_prompt_body(q_ref, kvf_ref, kv_ref, o_ref, m_ref, l_ref, acc_ref):
    i = pl.program_id(1)
    nkb = kvf_ref.shape[0] // QT
    q = _stack_heads(q_ref, 128)
    qb = q.astype(BF16)
    rows = q.shape[0]
    kmean = jnp.concatenate(
        [jnp.mean(kvf_ref[n * QT:(n + 1) * QT, 0:128], axis=0, keepdims=True) for n in range(nkb)], axis=0)
    gate_t = _dot3_nt(kmean, q)
    blk = lax.broadcasted_iota(I32, gate_t.shape, 0)
    sel_t = _topk_rows(jnp.where(blk < i, gate_t, -jnp.inf), MOBA_TOPK, axis=0)
    drop = jnp.concatenate([1.0 - sel_t, jnp.zeros((128 - nkb, rows), F32)], axis=0).T
    q_aug = jnp.concatenate([qb, drop.astype(BF16)], axis=1)
    lane = lax.broadcasted_iota(I32, (QT, 128), 1)
    _flash_init(m_ref, l_ref, acc_ref)

    def past_block(n, carry):
        k_aug = jnp.concatenate([_key_block(kv_ref, n, slice(0, 128)),
                                 jnp.where(lane == n, NEG, 0.0).astype(BF16)], axis=1)
        _flash_block(q_aug, k_aug, _key_block(kv_ref, n, slice(128, 256)), None, ATTN_SCALE,
                     m_ref, l_ref, acc_ref)
        return carry

    lax.fori_loop(0, i, past_block, 0)
    _flash_block(qb, _key_block(kv_ref, i, slice(0, 128)), _key_block(kv_ref, i, slice(128, 256)),
                 _rows(_causal_bias(), rows), ATTN_SCALE, m_ref, l_ref, acc_ref)
    _flash_finish(o_ref, l_ref, acc_ref)


def _mla_prompt_body(q_ref, kv_ref, o_ref, m_ref, l_ref, acc_ref):
    i = pl.program_id(1)
    q = _stack_heads(q_ref, MLA_KV_PAD)
    _flash_init(m_ref, l_ref, acc_ref)

    def past_block(n, carry):
        kv = _key_block(kv_ref, n)
        _flash_block(q, kv, kv[:, 0:MLA_KV_LORA], None, MLA_SCALE, m_ref, l_ref, acc_ref)
        return carry

    lax.fori_loop(0, i, past_block, 0)
    kv = _key_block(kv_ref, i)
    _flash_block(q, kv, kv[:, 0:MLA_KV_LORA], _rows(_causal_bias(), q.shape[0]), MLA_SCALE,
                 m_ref, l_ref, acc_ref)
    _flash_finish(o_ref, l_ref, acc_ref)


def _sb_block(q, k, v, strict, tri, carry_ref, acc_ref):
    z = _dot_nt(q, k) * ATTN_SCALE
    log_keep = -_softplus(z)
    if strict is not None:
        log_keep = jnp.where(strict, log_keep, 0.0)
    hi, lo = _split_bf16(log_keep)
    later = (_dot(hi, tri) + _dot(lo, tri)) + _lanes(carry_ref[...], QT)
    a = jnp.exp((z + log_keep) + later)
    if strict is not None:
        a = jnp.where(strict, a, 0.0)
    acc_ref[...] += _dot(a.astype(BF16), v)
    carry_ref[...] += _row_stat(jnp.sum(log_keep, axis=1, keepdims=True))


def _sb_prompt_body(q_ref, kv_ref, tri_ref, o_ref, carry_ref, acc_ref):
    i = pl.program_id(1)
    qb = _stack_heads(q_ref, 128).astype(BF16)
    carry_ref[...] = jnp.zeros(carry_ref.shape, F32)
    acc_ref[...] = jnp.zeros(acc_ref.shape, F32)
    tri = tri_ref[...]
    strict = _rows(_local_causal(True), qb.shape[0]) > 0.5
    _sb_block(qb, _key_block(kv_ref, i, slice(0, 128)), _key_block(kv_ref, i, slice(128, 256)),
              strict, tri, carry_ref, acc_ref)

    def past_block(t, carry):
        n = i - 1 - t
        _sb_block(qb, _key_block(kv_ref, n, slice(0, 128)), _key_block(kv_ref, n, slice(128, 256)),
                  None, tri, carry_ref, acc_ref)
        return carry

    lax.fori_loop(0, i, past_block, 0)
    _unstack_heads(o_ref, acc_ref[...])


def _dsa_prompt_body(q_ref, kv_ref, bias_ref, o_ref, m_ref, l_ref, acc_ref):
    i = pl.program_id(1)
    qb = _stack_heads(q_ref, 128).astype(BF16)
    rows = qb.shape[0]
    _flash_init(m_ref, l_ref, acc_ref)

    def block(n, carry):
        _flash_block(qb, _key_block(kv_ref, n, slice(0, 128)), _key_block(kv_ref, n, slice(128, 256)),
                     _rows(bias_ref[n], rows), ATTN_SCALE, m_ref, l_ref, acc_ref)
        return carry

    lax.fori_loop(0, i + 1, block, 0)
    _flash_finish(o_ref, l_ref, acc_ref)


def prompt_attention(kind, q, kv, n_batch, seq, extra=None):
    nqt = seq // QT
    nkb = seq // QT
    hps = HEADS_PER_STEP
    rows = hps * QT
    dq = q.shape[1] // N_HEADS
    dkv = kv.shape[1]
    q_spec = pl.BlockSpec((QT, hps * dq), lambda b, i, g: (b * nqt + i, g))
    kv_spec = pl.BlockSpec((seq, dkv), lambda b, i, g: (b, 0))
    stat = pltpu.VMEM((rows, 128), F32)
    if kind == "moba":
        body, dv, ins, specs = _moba_prompt_body, 128, (q, extra, kv), [q_spec, kv_spec, kv_spec]
        scratch = [stat, stat, pltpu.VMEM((rows, dv), F32)]
    elif kind == "mla":
        body, dv, ins, specs = _mla_prompt_body, MLA_KV_LORA, (q, kv), [q_spec, kv_spec]
        scratch = [stat, stat, pltpu.VMEM((rows, dv), F32)]
    elif kind == "sb":
        body, dv, ins = _sb_prompt_body, 128, (q, kv, extra)
        specs = [q_spec, kv_spec, pl.BlockSpec((QT, QT), lambda b, i, g: (0, 0))]
        scratch = [stat, pltpu.VMEM((rows, dv), F32)]
    else:
        body, dv, ins = _dsa_prompt_body, 128, (q, kv, extra)
        specs = [q_spec, kv_spec, pl.BlockSpec((nkb, QT, QT), lambda b, i, g: (0, b * nqt + i, 0))]
        scratch = [stat, stat, pltpu.VMEM((rows, dv), F32)]
    return pl.pallas_call(
        body,
        out_shape=jax.ShapeDtypeStruct((n_batch * seq, N_HEADS * dv), BF16),
        grid=(n_batch, nqt, N_HEADS // hps),
        in_specs=specs,
        out_specs=pl.BlockSpec((QT, hps * dv), lambda b, i, g: (b * nqt + i, g)),
        scratch_shapes=scratch,
        compiler_params=_cparams(("parallel", "parallel", "arbitrary"), VMEM_LIMIT_MB),
        name=kind + "_prompt",
    )(*ins)


def _idx_prompt_body(iq_ref, ikd_ref, iw_ref, o_ref):
    i = pl.program_id(1)
    n = pl.program_id(2)

    @pl.when(n <= i)
    def _():
        ikd = ikd_ref[...]
        lane = lax.broadcasted_iota(I32, (QT, 128), 1)
        acc = jnp.zeros((QT, QT), F32)
        for j in range(IDX_HEADS // 2):
            qp = iq_ref[:, j * 128:(j + 1) * 128]
            zero = jnp.zeros_like(qp)
            le = _dot_nt(jnp.where(lane < IDX_DIM, qp, zero), ikd)
            lo = _dot_nt(jnp.where(lane >= IDX_DIM, qp, zero), ikd)
            acc = acc + iw_ref[:, 2 * j:2 * j + 1] * jnp.maximum(le, 0.0)
            acc = acc + iw_ref[:, 2 * j + 1:2 * j + 2] * jnp.maximum(lo, 0.0)
        o_ref[...] = jnp.where(_diag_mask(i == n, _local_causal(False)), acc, -jnp.inf)

    @pl.when(n > i)
    def _():
        o_ref[...] = jnp.full(o_ref.shape, -jnp.inf, F32)


def idx_prompt_scores(iq, ikd, iw, n_batch, seq):
    nqt = seq // QT
    return pl.pallas_call(
        _idx_prompt_body,
        out_shape=jax.ShapeDtypeStruct((n_batch * seq, seq), F32),
        grid=(n_batch, nqt, nqt),
        in_specs=[pl.BlockSpec((QT, 2048), lambda b, i, n: (b * nqt + i, 0)),
                  pl.BlockSpec((QT, 128), lambda b, i, n: (b * nqt + jnp.minimum(n, i), 0)),
                  pl.BlockSpec((QT, 128), lambda b, i, n: (b * nqt + i, 0))],
        out_specs=pl.BlockSpec((QT, QT), lambda b, i, n: (b * nqt + i, n)),
        compiler_params=_cparams(("parallel", "parallel", "arbitrary"), VMEM_LIMIT_MB),
        name="idx_prompt_scores",
    )(iq, ikd, iw)


def _topk_mask_body(s_ref, pos_ref, o_ref, *, k, idx_bits):
    s = s_ref[...]
    bits = lax.bitcast_convert_type(s, I32)
    key = jnp.where(bits >= 0, bits, bits ^ jnp.int32(0x7FFFFFFF))
    col = lax.broadcasted_iota(I32, s.shape, 1)
    kf = jnp.float32(k)

    def count(mask):
        return jnp.sum(jnp.where(mask, 1.0, 0.0), axis=1, keepdims=True)

    t0 = jnp.where(count(key >= 0) >= kf, jnp.int32(0), jnp.int32(INT_MIN))

    def value_step(b, t):
        cand = t | lax.shift_left(jnp.int32(1), 30 - b)
        return jnp.where(count(key >= cand) >= kf, cand, t)

    thr = lax.fori_loop(0, 31, value_step, t0)
    gt = key > thr
    eq = key == thr
    need = kf - count(gt)
    n_eq = count(eq)

    def tie_step(b, m):
        cand = m | lax.shift_left(jnp.int32(1), idx_bits - 1 - b)
        c = count(jnp.logical_and(eq, col < cand))
        return jnp.where(c < need, cand, m)

    def tie_search():
        return lax.fori_loop(0, idx_bits, tie_step, jnp.zeros(thr.shape, I32))

    def tie_all():
        return jnp.full(thr.shape, 2 ** idx_bits - 1, I32)

    last = lax.cond(jnp.max(n_eq - need) > 0.5, tie_search, tie_all)
    sel = jnp.logical_or(gt, jnp.logical_and(eq, col <= last))
    sel = jnp.logical_and(sel, col <= pos_ref[...])
    bias = jnp.where(sel, 0.0, NEG)
    if len(o_ref.shape) == 2:
        o_ref[...] = bias
    else:
        for n in range(o_ref.shape[0]):
            o_ref[n] = bias[:, n * QT:(n + 1) * QT]


def topk_bias(scores, pos, k, tr, blocked):
    r, l = scores.shape
    idx_bits = max(1, (l - 1).bit_length())
    if blocked:
        out_shape = jax.ShapeDtypeStruct((l // QT, r, QT), F32)
        out_spec = pl.BlockSpec((l // QT, tr, QT), lambda i: (0, i, 0))
    else:
        out_shape = jax.ShapeDtypeStruct((r, l), F32)
        out_spec = pl.BlockSpec((tr, l), lambda i: (i, 0))
    return pl.pallas_call(
        functools.partial(_topk_mask_body, k=k, idx_bits=idx_bits),
        out_shape=out_shape,
        grid=(r // tr,),
        in_specs=[pl.BlockSpec((tr, l), lambda i: (i, 0)),
                  pl.BlockSpec((tr, 1), lambda i: (i, 0))],
        out_specs=out_spec,
        compiler_params=_cparams(("parallel",), VMEM_LIMIT_MB),
        name="topk_bias",
    )(scores, pos)


def _paged_kernel(pt_ref, cache_ref, *refs, body, n_in, n_out, layer, n_pages, lane_pages):
    ins = refs[:n_in]
    outs = refs[n_in:n_in + n_out]
    buf, sem = refs[n_in + n_out:]
    r = pl.program_id(0)
    nr = pl.num_programs(0)
    page_rows, w = cache_ref.shape[2:]

    def page_copy(page, p, slot):
        if lane_pages:
            dst = buf.at[slot, :, pl.ds(p * w, w)]
        else:
            dst = buf.at[slot, pl.ds(p * page_rows, page_rows)]
        return pltpu.make_async_copy(cache_ref.at[layer, page], dst, sem.at[slot])

    def start_request(req, slot):
        for p in range(n_pages):
            page_copy(pt_ref[req * n_pages + p], p, slot).start()

    @pl.when(r == 0)
    def _():
        start_request(0, 0)

    @pl.when(r + 1 < nr)
    def _():
        start_request(r + 1, (r + 1) % 2)

    slot = r % 2
    for p in range(n_pages):
        page_copy(0, p, slot).wait()
    body(buf.at[slot], *ins, *outs)


def paged_call(name, body, cache, layer, page_table, ins, in_blocks, out_shape, out_block, lane_pages=False):
    n_req, n_pages = page_table.shape
    page_rows, w = cache.shape[2:]
    window = (page_rows, n_pages * w) if lane_pages else (n_pages * page_rows, w)
    nd = lambda blk: (lambda r, pt: (r,) + (0,) * (len(blk) - 1))
    const = lambda blk: (lambda r, pt: (0,) * len(blk))
    in_specs = [pl.BlockSpec(memory_space=pl.ANY)]
    for blk, per_req in in_blocks:
        in_specs.append(pl.BlockSpec(blk, nd(blk) if per_req else const(blk)))
    grid_spec = pltpu.PrefetchScalarGridSpec(
        num_scalar_prefetch=1,
        grid=(n_req,),
        in_specs=in_specs,
        out_specs=pl.BlockSpec(out_block, nd(out_block)),
        scratch_shapes=[pltpu.VMEM((2,) + window, F32),
                        pltpu.SemaphoreType.DMA((2,))])
    return pl.pallas_call(
        functools.partial(_paged_kernel, body=body, n_in=len(ins), n_out=1, layer=layer, n_pages=n_pages,
                          lane_pages=lane_pages),
        out_shape=out_shape,
        grid_spec=grid_spec,
        compiler_params=_cparams(("arbitrary",), VMEM_LIMIT_MB),
        name=name,
    )(page_table.reshape(-1), cache, *ins)


def _softmax_with_new(s, s_new, v, v_new, v_keys_on_lanes=False):
    m = jnp.maximum(jnp.max(s, axis=1, keepdims=True), s_new)
    p = jnp.exp(s - m)
    p_new = jnp.exp(s_new - m)
    l = jnp.sum(p, axis=1, keepdims=True) + p_new
    pv = _dot_nt(p.astype(BF16), v) if v_keys_on_lanes else _dot(p.astype(BF16), v)
    acc = pv + p_new.astype(BF16).astype(F32) * v_new.astype(BF16).astype(F32)
    return acc / l


def _score_new(q, k_new):
    return jnp.sum(q.astype(BF16).astype(F32) * k_new.astype(BF16).astype(F32), axis=1, keepdims=True)


def _kv_rows(win):
    n = win.shape[0] // 2
    return win[pl.ds(0, n, stride=2), :], win[pl.ds(1, n, stride=2), :]


def _moba_decode_body(win, q_ref, new_ref, e_ref, o_ref):
    q = q_ref[0]
    k, v = _kv_rows(win)
    v = v.astype(BF16)
    nb = k.shape[0] // MOBA_BLOCK
    kmean = jnp.mean(k.reshape(nb, MOBA_BLOCK, 128), axis=1)
    sel = _topk_rows(_dot3_nt(q, kmean), MOBA_TOPK)
    picked = _dot(sel.astype(BF16), e_ref[...]) > 0.5
    s = jnp.where(picked, _dot_nt(q.astype(BF16), k.astype(BF16)) * ATTN_SCALE, NEG)
    new = new_ref[0]
    s_new = _score_new(q, new[:, 0:128]) * ATTN_SCALE
    o_ref[0] = _softmax_with_new(s, s_new, v, new[:, 128:256])


def _mla_decode_body(win, q_ref, new_ref, o_ref):
    q = q_ref[0][:, 0:MLA_KV_LORA + MLA_ROPE]
    kvt = win[...].astype(BF16)
    s = _dot(q, kvt) * MLA_SCALE
    new = new_ref[0]
    s_new = _score_new(q.astype(F32), new) * MLA_SCALE
    o_ref[0] = _softmax_with_new(s, s_new, kvt[0:MLA_KV_LORA, :], new[:, 0:MLA_KV_LORA],
                                 v_keys_on_lanes=True).astype(o_ref.dtype)


def _sb_decode_body(win, q_ref, o_ref):
    q = q_ref[0].astype(BF16)
    k, v = _kv_rows(win)
    k = k.astype(BF16)
    v = v.astype(BF16)
    z = _dot_nt(q, k) * ATTN_SCALE
    log_keep = -_softplus(z)
    n = z.shape[1]
    col = lax.broadcasted_iota(I32, z.shape, 1)
    suffix = log_keep
    d = 1
    while d < n:
        suffix = suffix + jnp.where(col + d < n, pltpu.roll(suffix, n - d, 1), 0.0)
        d *= 2
    a = jnp.exp((z + log_keep) + (suffix - log_keep))
    o_ref[0] = _dot(a.astype(BF16), v)


def _idx_decode_body(win, iq_ref, iw_ref, new_ref, o_ref):
    iq = iq_ref[0]
    w = iw_ref[0]
    logits = _dot(iq, win[...].astype(BF16))
    score = jnp.sum(w * jnp.maximum(logits, 0.0), axis=0, keepdims=True)
    l_new = _score_new(iq.astype(F32), new_ref[0])
    s_new = jnp.sum(w * jnp.maximum(l_new, 0.0), axis=0, keepdims=True)
    n = score.shape[1]
    o_ref[0, :, 0:n] = score
    lane = lax.broadcasted_iota(I32, (1, 128), 1)
    o_ref[0, :, n:n + 128] = jnp.where(lane == 0, s_new, -jnp.inf)


def _dsa_decode_body(win, q_ref, new_ref, bias_ref, o_ref):
    q = q_ref[0]
    k, v = _kv_rows(win)
    k = k.astype(BF16)
    v = v.astype(BF16)
    n = k.shape[0]
    s = _dot_nt(q.astype(BF16), k) * ATTN_SCALE + bias_ref[0, :, 0:n]
    new = new_ref[0]
    s_new = _score_new(q, new[:, 0:128]) * ATTN_SCALE + bias_ref[0, :, n:n + 1]
    o_ref[0] = _softmax_with_new(s, s_new, v, new[:, 128:256])


def _rope_tables(pos):
    posf = pos.astype(F32)[:, None]
    inv128 = ROPE_THETA ** (-2.0 * jnp.arange(64, dtype=F32) / 128)
    ang = posf * inv128[None, :]
    c, s = jnp.cos(ang), jnp.sin(ang)
    tabs128 = (jnp.concatenate([c, c], axis=1), jnp.concatenate([-s, s], axis=1))
    inv64 = ROPE_THETA ** (-2.0 * jnp.arange(32, dtype=F32) / 64)
    ang = posf * inv64[None, :]
    c, s = jnp.cos(ang), jnp.sin(ang)
    z = jnp.zeros_like(s)
    tabs64 = (jnp.concatenate([c, c, c, c], axis=1),
              jnp.concatenate([-s, z, -s, z], axis=1),
              jnp.concatenate([z, s, z, s], axis=1))
    return tabs128, tabs64


def _pad_cols(w, n):
    return jnp.pad(w, ((0, 0), (0, n - w.shape[1])))


def _even_w_in(w):
    return _pad_cols(w, EVEN_PAD).astype(BF16)


def _odd_w_in(w):
    return jnp.concatenate([w[:, 0:6656], _pad_cols(w[:, 6656:6688], 128), _pad_cols(w[:, 6688:6752], 128)],
                           axis=1).astype(BF16)


def _mla_w_uq(w):
    w3 = w.reshape(w.shape[0], N_HEADS, MLA_QK)
    nope = w3[:, :, 0:128].reshape(w.shape[0], N_HEADS * 128)
    rope = jnp.pad(w3[:, :, 128:], ((0, 0), (0, 0), (0, 64))).reshape(w.shape[0], N_HEADS * 128)
    return jnp.concatenate([nope, rope], axis=1).astype(BF16)


def _block_membership(n_keys):
    blk = jnp.arange(n_keys, dtype=I32)[None, :] // MOBA_BLOCK
    return (blk == jnp.arange(n_keys // MOBA_BLOCK, dtype=I32)[:, None]).astype(BF16)


def _kv_row_view(cache):
    return cache.reshape(cache.shape[0], cache.shape[1], 2 * cache.shape[2], cache.shape[4])


def _even_mixer(h, x, j, dims, tabs, caches, page_table, w):
    n_batch, seq, n_req, tm, tr = dims
    n_prompt = n_batch * seq
    tabs128, tabs64 = tabs
    cache_moba, cache_mla = caches
    p = matmul(h, _even_w_in(w["even_w_in"][j]), F32, tm // 2, EVEN_PAD // 7)
    mq, moba_new, moba_bf, cqn, mla_new, kv_pad = even_post(p, tabs128, tabs64, w["mla_q_norm_w"][j],
                                                            w["mla_kv_norm_w"][j], tr)
    qfull = matmul(cqn, _mla_w_uq(w["mla_w_uq"][j]), F32, tm, 512)
    w_uk_t = jnp.transpose(w["mla_w_uk"][j], (1, 2, 0)).astype(BF16)
    w_uv = jnp.transpose(w["mla_w_uv"][j], (1, 0, 2)).astype(BF16)
    q_cat = mla_qcat(qfull, w_uk_t, tabs64, tm)

    moba_p = prompt_attention("moba", mq, moba_bf, n_batch, seq, moba_new)
    mla_p = prompt_attention("mla", q_cat, kv_pad, n_batch, seq)

    n_past = page_table.shape[1] * PAGE_SIZE
    moba_s = paged_call(
        "moba_decode", _moba_decode_body, _kv_row_view(cache_moba), j, page_table,
        (mq[n_prompt:].reshape(n_req, N_HEADS, 128), moba_new[n_prompt:].reshape(n_req, 1, 256),
         _block_membership(n_past)),
        [((1, N_HEADS, 128), True), ((1, 1, 256), True), ((n_past // MOBA_BLOCK, n_past), False)],
        jax.ShapeDtypeStruct((n_req, N_HEADS, 128), F32), (1, N_HEADS, 128))
    mla_s = paged_call(
        "mla_decode", _mla_decode_body, jnp.swapaxes(cache_mla, 2, 3), j, page_table,
        (q_cat[n_prompt:].reshape(n_req, N_HEADS, MLA_KV_PAD), mla_new[n_prompt:].reshape(n_req, 1, -1)),
        [((1, N_HEADS, MLA_KV_PAD), True), ((1, 1, MLA_KV_LORA + MLA_ROPE), True)],
        jax.ShapeDtypeStruct((n_req, N_HEADS, MLA_KV_LORA), BF16), (1, N_HEADS, MLA_KV_LORA),
        lane_pages=True)

    moba_out = jnp.concatenate([moba_p, moba_s.reshape(n_req, -1).astype(BF16)], axis=0)
    mla_lat = jnp.concatenate([mla_p, mla_s.reshape(n_req, -1)], axis=0)
    mla_out = headwise_matmul(mla_lat, w_uv, BF16, tm)
    x = matmul2_residual(moba_out, mla_out, w["even_w_out"], (j,), x, tm, 256)
    return x, (moba_new, mla_new)


def _odd_mixer(h, x, j, dims, tabs, caches, page_table, w, consts):
    n_batch, seq, n_req, tm, tr = dims
    n_prompt = n_batch * seq
    tabs128, tabs64 = tabs
    cache_sb, cache_dsa, cache_idx = caches
    tri, pos_prompt, pos_sample = consts
    p = matmul(h, _odd_w_in(w["odd_w_in"][j]), F32, tm // 2, ODD_PAD // 9)
    sq, sb_new, sb_bf, dq, dsa_new, dsa_bf, iq, iw, idx_new, ikd = odd_post(p, tabs128, tabs64, tr)

    sb_p = prompt_attention("sb", sq, sb_bf, n_batch, seq, tri)
    scores_p = idx_prompt_scores(iq, ikd, iw, n_batch, seq)
    bias_p = topk_bias(scores_p, pos_prompt, min(DSA_TOPK, seq // 4), QT, blocked=True)
    dsa_p = prompt_attention("dsa", dq, dsa_bf, n_batch, seq, bias_p)

    n_past = page_table.shape[1] * PAGE_SIZE
    sb_s = paged_call(
        "sb_decode", _sb_decode_body, _kv_row_view(cache_sb), j, page_table,
        (sq[n_prompt:].reshape(n_req, N_HEADS, 128),),
        [((1, N_HEADS, 128), True)],
        jax.ShapeDtypeStruct((n_req, N_HEADS, 128), F32), (1, N_HEADS, 128))
    scores_s = paged_call(
        "idx_decode", _idx_decode_body, jnp.swapaxes(cache_idx, 2, 3), j, page_table,
        (iq[n_prompt:].reshape(n_req, IDX_HEADS, IDX_DIM), iw[n_prompt:, 0:IDX_HEADS].reshape(n_req, IDX_HEADS, 1),
         idx_new[n_prompt:].reshape(n_req, 1, IDX_DIM)),
        [((1, IDX_HEADS, IDX_DIM), True), ((1, IDX_HEADS, 1), True), ((1, 1, IDX_DIM), True)],
        jax.ShapeDtypeStruct((n_req, 1, n_past + 128), F32), (1, 1, n_past + 128), lane_pages=True)
    bias_s = topk_bias(scores_s.reshape(n_req, n_past + 128), pos_sample, min(DSA_TOPK, (n_past + 1) // 4),
                       n_req, blocked=False)
    dsa_s = paged_call(
        "dsa_decode", _dsa_decode_body, _kv_row_view(cache_dsa), j, page_table,
        (dq[n_prompt:].reshape(n_req, N_HEADS, 128), dsa_new[n_prompt:].reshape(n_req, 1, 256),
         bias_s.reshape(n_req, 1, n_past + 128)),
        [((1, N_HEADS, 128), True), ((1, 1, 256), True), ((1, 1, n_past + 128), True)],
        jax.ShapeDtypeStruct((n_req, N_HEADS, 128), F32), (1, N_HEADS, 128))

    sb_out = jnp.concatenate([sb_p, sb_s.reshape(n_req, -1).astype(BF16)], axis=0)
    dsa_out = jnp.concatenate([dsa_p, dsa_s.reshape(n_req, -1).astype(BF16)], axis=0)
    x = matmul2_residual(sb_out, dsa_out, w["odd_w_out"], (j,), x, tm, 256)
    return x, (sb_new, dsa_new, idx_new)


def _trunk(x_prompt, x_sample, caches, page_table, w, tm, tr):
    n_batch, seq, d = x_prompt.shape
    n_req = x_sample.shape[0]
    n_prompt = n_batch * seq
    n_past = page_table.shape[1] * PAGE_SIZE
    depth = w["norm_w"].shape[0]
    dims = (n_batch, seq, n_req, tm, tr)
    x = jnp.concatenate([x_prompt.reshape(n_prompt, d), x_sample.reshape(n_req, d)], axis=0)
    pos = jnp.concatenate([jnp.tile(jnp.arange(seq, dtype=I32), n_batch), jnp.full((n_req,), n_past, I32)])
    tabs = _rope_tables(pos)
    tri = (jnp.arange(QT)[:, None] > jnp.arange(QT)[None, :]).astype(BF16)
    consts = (tri, pos[:n_prompt, None], pos[n_prompt:, None])
    cache_moba, cache_mla, cache_sb, cache_dsa, cache_idx = caches
    states = []
    tn = 256
    for layer in range(depth):
        j = layer // 2
        nw = w["norm_w"][layer]
        h = rmsnorm(x, nw[0], BF16, tr)
        a = ffn_up(h, w["ffn_w_gate"], w["ffn_w_up"], (layer, 0), tm, tn)
        x = matmul_residual(a, w["ffn_w_down"], (layer, 0), x, 0.5, tm, tn)
        h = rmsnorm(x, nw[1], BF16, tr)
        if layer % 2 == 0:
            x, st = _even_mixer(h, x, j, dims, tabs, (cache_moba, cache_mla), page_table, w)
        else:
            x, st = _odd_mixer(h, x, j, dims, tabs, (cache_sb, cache_dsa, cache_idx), page_table, w, consts)
        states.append(st)
        h = rmsnorm(x, nw[2], BF16, tr)
        a = ffn_up(h, w["ffn_w_gate"], w["ffn_w_up"], (layer, 1), tm, tn)
        x = matmul_residual(a, w["ffn_w_down"], (layer, 1), x, 0.5, tm, tn)
    y = rmsnorm(x, w["final_norm_w"], F32, tr)

    def split(rows, tail):
        a = jnp.stack(rows)
        return (a[:, :n_prompt].reshape(a.shape[0], n_batch, seq, *tail),
                a[:, n_prompt:].reshape(a.shape[0], n_req, 1, *tail))

    even = states[0::2]
    odd = states[1::2]
    moba_p, moba_s = split([s[0] for s in even], (2, HEAD_DIM))
    mla_p, mla_s = split([s[1] for s in even], (MLA_KV_LORA + MLA_ROPE,))
    sb_p, sb_s = split([s[0] for s in odd], (2, HEAD_DIM))
    dsa_p, dsa_s = split([s[1] for s in odd], (2, HEAD_DIM))
    idx_p, idx_s = split([s[2] for s in odd], (IDX_DIM,))
    return (y[:n_prompt].reshape(n_batch, seq, d), y[n_prompt:].reshape(n_req, 1, d),
            moba_p, moba_s, mla_p, mla_s, sb_p, sb_s, dsa_p, dsa_s, idx_p, idx_s)


def kernel(x_prompt, x_sample, cache_moba_kv, cache_mla_latent, cache_sb_kv, cache_dsa_kv, cache_dsa_idx_k, page_table, norm_w, ffn_w_gate, ffn_w_up, ffn_w_down, even_w_in, even_w_out, mla_q_norm_w, mla_w_uq, mla_kv_norm_w, mla_w_uk, mla_w_uv, odd_w_in, odd_w_out, final_norm_w):
    w = dict(norm_w=norm_w, ffn_w_gate=ffn_w_gate, ffn_w_up=ffn_w_up, ffn_w_down=ffn_w_down,
             even_w_in=even_w_in, even_w_out=even_w_out, mla_q_norm_w=mla_q_norm_w, mla_w_uq=mla_w_uq,
             mla_kv_norm_w=mla_kv_norm_w, mla_w_uk=mla_w_uk, mla_w_uv=mla_w_uv,
             odd_w_in=odd_w_in, odd_w_out=odd_w_out, final_norm_w=final_norm_w)
    caches = (cache_moba_kv, cache_mla_latent, cache_sb_kv, cache_dsa_kv, cache_dsa_idx_k)
    n_tokens = x_prompt.shape[0] * x_prompt.shape[1] + x_sample.shape[0]
    return _trunk(x_prompt, x_sample, caches, page_table, w, n_tokens // 4, n_tokens // 40)
```

```python
import functools

import jax
import jax.numpy as jnp
from jax import lax
from jax.experimental import pallas as pl
from jax.experimental.pallas import tpu as pltpu

F32 = jnp.float32
BF16 = jnp.bfloat16
I32 = jnp.int32

D_MODEL = 4096
HEAD_DIM = 128
N_HEADS = 16
MOBA_BLOCK = 256
MOBA_TOPK = 3
MLA_Q_LORA = 768
MLA_KV_LORA = 256
MLA_ROPE = 64
MLA_QK = 192
IDX_HEADS = 32
IDX_DIM = 64
DSA_TOPK = 256
PAGE_SIZE = 128
ROPE_THETA = 10000.0
EPS = 1e-6
ATTN_SCALE = HEAD_DIM ** -0.5
MLA_SCALE = MLA_QK ** -0.5
IDX_SCALE = (IDX_HEADS * IDX_DIM) ** -0.5
NEG = -1e30
INT_MIN = -2 ** 31

EVEN_PAD = 3584
ODD_PAD = 6912
MLA_KV_PAD = 384
QT = 256
HEADS_PER_STEP = 16
VMEM_LIMIT_MB = 52


def _cparams(sem, vmem_mb=None):
    return pltpu.CompilerParams(
        dimension_semantics=sem,
        vmem_limit_bytes=None if vmem_mb is None else vmem_mb * 2 ** 20)


def _dot(a, b):
    return jnp.dot(a, b, preferred_element_type=F32)


def _dot_nt(a, b):
    return lax.dot_general(a, b, (((1,), (1,)), ((), ())), preferred_element_type=F32)


def _split_bf16(x):
    hi = x.astype(BF16)
    lo = (x - hi.astype(F32)).astype(BF16)
    return hi, lo


def _dot3_nt(a, b):
    ah, al = _split_bf16(a)
    bh, bl = _split_bf16(b)
    return _dot_nt(ah, bh) + (_dot_nt(ah, bl) + _dot_nt(al, bh))


def _softplus(z):
    return jnp.maximum(z, 0.0) + jnp.log(1.0 + jnp.exp(-jnp.abs(z)))


def _rope128(x, cos, sin_signed):
    return x * cos + pltpu.roll(x, 64, 1) * sin_signed


def _rope64(x, cos, sin_a, sin_b):
    return x * cos + pltpu.roll(x, 96, 1) * sin_a + pltpu.roll(x, 32, 1) * sin_b


def _topk_rows(g, k, axis=1):
    col = lax.broadcasted_iota(I32, g.shape, axis).astype(F32)
    big = jnp.float32(g.shape[axis])
    sel = jnp.zeros(g.shape, F32)
    for _ in range(k):
        mx = jnp.max(g, axis=axis, keepdims=True)
        cand = jnp.where(g == mx, col, big)
        cand = jnp.where(mx > -jnp.inf, cand, big)
        first = jnp.min(cand, axis=axis, keepdims=True)
        pick = col == first
        sel = jnp.where(pick, 1.0, sel)
        g = jnp.where(pick, -jnp.inf, g)
    return sel


def _rmsnorm_body(x_ref, w_ref, o_ref):
    x = x_ref[...]
    ms = jnp.mean(x * x, axis=-1, keepdims=True)
    o_ref[...] = ((x * lax.rsqrt(ms + EPS)) * w_ref[...]).astype(o_ref.dtype)


def rmsnorm(x, w, out_dtype, tm):
    m, d = x.shape
    return pl.pallas_call(
        _rmsnorm_body,
        out_shape=jax.ShapeDtypeStruct((m, d), out_dtype),
        grid=(m // tm,),
        in_specs=[pl.BlockSpec((tm, d), lambda i: (i, 0)),
                  pl.BlockSpec((1, d), lambda i: (0, 0))],
        out_specs=pl.BlockSpec((tm, d), lambda i: (i, 0)),
        compiler_params=_cparams(("parallel",), VMEM_LIMIT_MB),
        name="rmsnorm",
    )(x, w.reshape(1, d))


def _row_tile_spec(tm, k):
    return pl.BlockSpec((tm, k), lambda i, j: (i, 0), pipeline_mode=pl.Buffered(1))


def _mm_body(a_ref, b_ref, o_ref):
    o_ref[...] = _dot(a_ref[...].astype(BF16), b_ref[...].astype(BF16)).astype(o_ref.dtype)


def matmul(a, b, out_dtype, tm, tn):
    m, k = a.shape
    n = b.shape[1]
    return pl.pallas_call(
        _mm_body,
        out_shape=jax.ShapeDtypeStruct((m, n), out_dtype),
        grid=(m // tm, n // tn),
        in_specs=[_row_tile_spec(tm, k),
                  pl.BlockSpec((k, tn), lambda i, j: (0, j))],
        out_specs=pl.BlockSpec((tm, tn), lambda i, j: (i, j)),
        compiler_params=_cparams(("parallel", "parallel"), VMEM_LIMIT_MB),
        name="matmul",
    )(a, b)


def _weight_spec(w, lead, tn):
    k = w.shape[-2]
    return pl.BlockSpec((None,) * len(lead) + (k, tn), lambda i, j: tuple(lead) + (0, j))


def _row_scale(ss_ref, d):
    return lax.rsqrt(jnp.sum(ss_ref[...], axis=1, keepdims=True) * (1.0 / d) + EPS)


def _normed_specs(tm, k):
    return [_row_tile_spec(tm, k), pl.BlockSpec((tm, 128), lambda i, j: (i, 0))]


def _ffn_up_body(xg_ref, ss_ref, g_ref, u_ref, o_ref):
    a = xg_ref[...]
    r = _row_scale(ss_ref, a.shape[1])
    g = r * _dot(a, g_ref[...].astype(BF16))
    u = r * _dot(a, u_ref[...].astype(BF16))
    o_ref[...] = ((g * (1.0 / (1.0 + jnp.exp(-g)))) * u).astype(o_ref.dtype)


def ffn_up(stream, w_gate, w_up, lead, tm, tn):
    _, xg, ss = stream
    m, k = xg.shape
    n = w_gate.shape[-1]
    return pl.pallas_call(
        _ffn_up_body,
        out_shape=jax.ShapeDtypeStruct((m, n), BF16),
        grid=(m // tm, n // tn),
        in_specs=_normed_specs(tm, k) + [_weight_spec(w_gate, lead, tn), _weight_spec(w_up, lead, tn)],
        out_specs=pl.BlockSpec((tm, tn), lambda i, j: (i, j)),
        compiler_params=_cparams(("parallel", "parallel"), VMEM_LIMIT_MB),
        name="ffn_up",
    )(xg, ss, w_gate, w_up)


def _mm_normed_body(xg_ref, ss_ref, b_ref, o_ref):
    a = xg_ref[...]
    o_ref[...] = (_row_scale(ss_ref, a.shape[1]) * _dot(a, b_ref[...])).astype(o_ref.dtype)


def matmul_normed(stream, b, out_dtype, tm, tn):
    _, xg, ss = stream
    m, k = xg.shape
    n = b.shape[1]
    return pl.pallas_call(
        _mm_normed_body,
        out_shape=jax.ShapeDtypeStruct((m, n), out_dtype),
        grid=(m // tm, n // tn),
        in_specs=_normed_specs(tm, k) + [pl.BlockSpec((k, tn), lambda i, j: (0, j))],
        out_specs=pl.BlockSpec((tm, tn), lambda i, j: (i, j)),
        compiler_params=_cparams(("parallel", "parallel"), VMEM_LIMIT_MB),
        name="matmul_normed",
    )(xg, ss, b)


def _stream_prep_body(x_ref, gain_ref, xg_ref, ss_ref):
    x = x_ref[...]
    xg_ref[...] = (x * gain_ref[...]).astype(xg_ref.dtype)
    _square_sums(x, ss_ref, first=True)


def _square_sums(x, ss_ref, first):
    sq = x * x
    part = sq[:, 0:128]
    for c in range(1, x.shape[1] // 128):
        part = part + sq[:, c * 128:(c + 1) * 128]
    if first is True:
        ss_ref[...] = part
    else:
        @pl.when(first)
        def _():
            ss_ref[...] = part

        @pl.when(jnp.logical_not(first))
        def _():
            ss_ref[...] += part


def stream_prep(x, gain, tr):
    m, d = x.shape
    return pl.pallas_call(
        _stream_prep_body,
        out_shape=(jax.ShapeDtypeStruct((m, d), BF16), jax.ShapeDtypeStruct((m, 128), F32)),
        grid=(m // tr,),
        in_specs=[pl.BlockSpec((tr, d), lambda i: (i, 0)), pl.BlockSpec((1, d), lambda i: (0, 0))],
        out_specs=(pl.BlockSpec((tr, d), lambda i: (i, 0)), pl.BlockSpec((tr, 128), lambda i: (i, 0))),
        compiler_params=_cparams(("parallel",), VMEM_LIMIT_MB),
        name="stream_prep",
    )(x, gain.reshape(1, d))


def _stream_out(m, n, tm, tn):
    shapes = (jax.ShapeDtypeStruct((m, n), F32), jax.ShapeDtypeStruct((m, n), BF16),
              jax.ShapeDtypeStruct((m, 128), F32))
    tile = pl.BlockSpec((tm, tn), lambda i, j: (i, j))
    return shapes, (tile, tile, pl.BlockSpec((tm, 128), lambda i, j: (i, 0)))


def _gain_spec(tn):
    return pl.BlockSpec((1, tn), lambda i, j: (0, j))


def _emit_stream(new, gain_ref, o_ref, og_ref, ss_ref):
    o_ref[...] = new
    og_ref[...] = (new * gain_ref[...]).astype(og_ref.dtype)
    _square_sums(new, ss_ref, first=pl.program_id(1) == 0)


def _mm_res_body(a_ref, b_ref, x_ref, gain_ref, o_ref, og_ref, ss_ref, *, scale):
    acc = _dot(a_ref[...], b_ref[...].astype(BF16))
    _emit_stream(x_ref[...] + scale * acc, gain_ref, o_ref, og_ref, ss_ref)


def matmul_residual(a, b, lead, x, scale, next_gain, tm, tn):
    m, k = a.shape
    n = b.shape[-1]
    out_shape, out_specs = _stream_out(m, n, tm, tn)
    return pl.pallas_call(
        functools.partial(_mm_res_body, scale=scale),
        out_shape=out_shape,
        grid=(m // tm, n // tn),
        in_specs=[_row_tile_spec(tm, k),
                  _weight_spec(b, lead, tn),
                  pl.BlockSpec((tm, tn), lambda i, j: (i, j)),
                  _gain_spec(tn)],
        out_specs=out_specs,
        compiler_params=_cparams(("parallel", "arbitrary"), VMEM_LIMIT_MB),
        name="matmul_residual",
    )(a, b, x, next_gain.reshape(1, n))


def _mm2_res_body(a1_ref, a2_ref, b_ref, x_ref, gain_ref, o_ref, og_ref, ss_ref):
    k1 = a1_ref.shape[1]
    acc = _dot(a1_ref[...], b_ref[:k1, :].astype(BF16)) + _dot(a2_ref[...], b_ref[k1:, :].astype(BF16))
    _emit_stream(x_ref[...] + acc, gain_ref, o_ref, og_ref, ss_ref)


def matmul2_residual(a1, a2, b, lead, x, next_gain, tm, tn):
    m, k1 = a1.shape
    k2 = a2.shape[1]
    n = b.shape[-1]
    out_shape, out_specs = _stream_out(m, n, tm, tn)
    return pl.pallas_call(
        _mm2_res_body,
        out_shape=out_shape,
        grid=(m // tm, n // tn),
        in_specs=[_row_tile_spec(tm, k1),
                  _row_tile_spec(tm, k2),
                  _weight_spec(b, lead, tn),
                  pl.BlockSpec((tm, tn), lambda i, j: (i, j)),
                  _gain_spec(tn)],
        out_specs=out_specs,
        compiler_params=_cparams(("parallel", "arbitrary"), VMEM_LIMIT_MB),
        name="matmul2_residual",
    )(a1, a2, b, x, next_gain.reshape(1, n))


def _headwise_body(x_ref, w_ref, o_ref):
    o_ref[...] = _dot(x_ref[...].astype(BF16), w_ref[0]).astype(o_ref.dtype)


def headwise_matmul(x, w, out_dtype, tm):
    m = x.shape[0]
    nh, din, dout = w.shape
    return pl.pallas_call(
        _headwise_body,
        out_shape=jax.ShapeDtypeStruct((m, nh * dout), out_dtype),
        grid=(m // tm, nh),
        in_specs=[pl.BlockSpec((tm, din), lambda i, h: (i, h)),
                  pl.BlockSpec((1, din, dout), lambda i, h: (h, 0, 0))],
        out_specs=pl.BlockSpec((tm, dout), lambda i, h: (i, h)),
        compiler_params=_cparams(("parallel", "parallel"), VMEM_LIMIT_MB),
        name="headwise_matmul",
    )(x, w)


def _mla_qcat_body(qn_ref, qr_ref, w_ref, c_ref, sa_ref, sb_ref, o_ref):
    lat = _dot(qn_ref[...].astype(BF16), w_ref[0])
    rot = _rope64(qr_ref[...], c_ref[...], sa_ref[...], sb_ref[...])
    o_ref[:, :MLA_KV_LORA] = lat.astype(o_ref.dtype)
    o_ref[:, MLA_KV_LORA:] = rot.astype(o_ref.dtype)


def mla_qcat(q, w_uk_t, tabs64, tm):
    m = q.shape[0]
    c64, sa, sb = tabs64
    tab = pl.BlockSpec((tm, 128), lambda i, h: (i, 0))
    return pl.pallas_call(
        _mla_qcat_body,
        out_shape=jax.ShapeDtypeStruct((m, N_HEADS * MLA_KV_PAD), BF16),
        grid=(m // tm, N_HEADS),
        in_specs=[pl.BlockSpec((tm, 128), lambda i, h: (i, h)),
                  pl.BlockSpec((tm, 128), lambda i, h: (i, N_HEADS + h)),
                  pl.BlockSpec((1, 128, MLA_KV_LORA), lambda i, h: (h, 0, 0)),
                  tab, tab, tab],
        out_specs=pl.BlockSpec((tm, MLA_KV_PAD), lambda i, h: (i, h)),
        compiler_params=_cparams(("parallel", "parallel"), VMEM_LIMIT_MB),
        name="mla_qcat",
    )(q, q, w_uk_t, c64, sa, sb)


def _even_post_body(p_ref, cos_ref, sin_ref, c64_ref, sa_ref, sb_ref, qnw_ref, kvnw_ref,
                    mq_ref, moba_ref, moba_bf_ref, cqn_ref, mla_ref, kvp_ref):
    cos = cos_ref[...]
    sin = sin_ref[...]
    for h in range(N_HEADS):
        sl = slice(h * 128, (h + 1) * 128)
        mq_ref[:, sl] = _rope128(p_ref[:, sl], cos, sin)
    moba_ref[:, 0:128] = _rope128(p_ref[:, 2048:2176], cos, sin)
    moba_ref[:, 128:256] = p_ref[:, 2176:2304]
    moba_bf_ref[...] = moba_ref[...].astype(moba_bf_ref.dtype)
    cq = p_ref[:, 2304:3072]
    ms = jnp.mean(cq * cq, axis=-1, keepdims=True)
    cqn_ref[...] = ((cq * lax.rsqrt(ms + EPS)) * qnw_ref[...]).astype(cqn_ref.dtype)
    ckv = p_ref[:, 3072:3328]
    ms = jnp.mean(ckv * ckv, axis=-1, keepdims=True)
    ckvn = (ckv * lax.rsqrt(ms + EPS)) * kvnw_ref[...]
    kr = _rope64(p_ref[:, 3328:3456], c64_ref[...], sa_ref[...], sb_ref[...])
    mla_ref[:, 0:MLA_KV_LORA] = ckvn
    mla_ref[:, MLA_KV_LORA:MLA_KV_LORA + MLA_ROPE] = kr[:, 0:MLA_ROPE]
    kvp_ref[:, 0:MLA_KV_LORA] = ckvn.astype(kvp_ref.dtype)
    kvp_ref[:, MLA_KV_LORA:] = kr.astype(kvp_ref.dtype)


def even_post(p, tabs128, tabs64, q_norm_w, kv_norm_w, tm):
    m = p.shape[0]
    row = lambda w: pl.BlockSpec((tm, w), lambda i: (i, 0))
    return pl.pallas_call(
        _even_post_body,
        out_shape=(jax.ShapeDtypeStruct((m, 2048), F32),
                   jax.ShapeDtypeStruct((m, 256), F32),
                   jax.ShapeDtypeStruct((m, 256), BF16),
                   jax.ShapeDtypeStruct((m, MLA_Q_LORA), BF16),
                   jax.ShapeDtypeStruct((m, MLA_KV_LORA + MLA_ROPE), F32),
                   jax.ShapeDtypeStruct((m, MLA_KV_PAD), BF16)),
        grid=(m // tm,),
        in_specs=[row(EVEN_PAD), row(128), row(128), row(128), row(128), row(128),
                  pl.BlockSpec((1, MLA_Q_LORA), lambda i: (0, 0)),
                  pl.BlockSpec((1, MLA_KV_LORA), lambda i: (0, 0))],
        out_specs=(row(2048), row(256), row(256), row(MLA_Q_LORA), row(MLA_KV_LORA + MLA_ROPE),
                   row(MLA_KV_PAD)),
        compiler_params=_cparams(("parallel",), VMEM_LIMIT_MB),
        name="even_post",
    )(p, *tabs128, *tabs64, q_norm_w.reshape(1, -1), kv_norm_w.reshape(1, -1))


def _odd_post_body(p_ref, cos_ref, sin_ref, c64_ref, sa_ref, sb_ref,
                   sq_ref, sb_new_ref, sb_bf_ref, dq_ref, dsa_new_ref, dsa_bf_ref,
                   iq_ref, iw_ref, idx_new_ref, ikd_ref):
    cos = cos_ref[...]
    sin = sin_ref[...]
    c64 = c64_ref[...]
    sa = sa_ref[...]
    sb = sb_ref[...]
    sq_ref[...] = p_ref[:, 0:2048]
    sb_new_ref[...] = p_ref[:, 2048:2304]
    sb_bf_ref[...] = p_ref[:, 2048:2304].astype(sb_bf_ref.dtype)
    for h in range(N_HEADS):
        sl = slice(h * 128, (h + 1) * 128)
        dq_ref[:, sl] = _rope128(p_ref[:, 2304 + h * 128:2304 + (h + 1) * 128], cos, sin)
        iq_ref[:, sl] = _rope64(p_ref[:, 4608 + h * 128:4608 + (h + 1) * 128], c64, sa, sb).astype(iq_ref.dtype)
    dsa_new_ref[:, 0:128] = _rope128(p_ref[:, 4352:4480], cos, sin)
    dsa_new_ref[:, 128:256] = p_ref[:, 4480:4608]
    dsa_bf_ref[...] = dsa_new_ref[...].astype(dsa_bf_ref.dtype)
    iw_ref[...] = p_ref[:, 6656:6784] * IDX_SCALE
    ik = _rope64(p_ref[:, 6784:6912], c64, sa, sb)
    idx_new_ref[...] = ik[:, 0:IDX_DIM]
    ikd_ref[...] = (ik + pltpu.roll(ik, 64, 1)).astype(ikd_ref.dtype)


def odd_post(p, tabs128, tabs64, tm):
    m = p.shape[0]
    row = lambda w: pl.BlockSpec((tm, w), lambda i: (i, 0))
    return pl.pallas_call(
        _odd_post_body,
        out_shape=(jax.ShapeDtypeStruct((m, 2048), F32),
                   jax.ShapeDtypeStruct((m, 256), F32),
                   jax.ShapeDtypeStruct((m, 256), BF16),
                   jax.ShapeDtypeStruct((m, 2048), F32),
                   jax.ShapeDtypeStruct((m, 256), F32),
                   jax.ShapeDtypeStruct((m, 256), BF16),
                   jax.ShapeDtypeStruct((m, 2048), BF16),
                   jax.ShapeDtypeStruct((m, 128), F32),
                   jax.ShapeDtypeStruct((m, IDX_DIM), F32),
                   jax.ShapeDtypeStruct((m, 128), BF16)),
        grid=(m // tm,),
        in_specs=[row(ODD_PAD), row(128), row(128), row(128), row(128), row(128)],
        out_specs=(row(2048), row(256), row(256), row(2048), row(256), row(256), row(2048), row(128),
                   row(IDX_DIM), row(128)),
        compiler_params=_cparams(("parallel",), VMEM_LIMIT_MB),
        name="odd_post",
    )(p, *tabs128, *tabs64)


def _flash_init(m_ref, l_ref, acc_ref):
    m_ref[...] = jnp.full(m_ref.shape, NEG, F32)
    l_ref[...] = jnp.zeros(l_ref.shape, F32)
    acc_ref[...] = jnp.zeros(acc_ref.shape, F32)


def _stack_heads(q_ref, d):
    return jnp.concatenate([q_ref[:, h * d:(h + 1) * d] for h in range(q_ref.shape[1] // d)], axis=0)


def _unstack_heads(o_ref, x):
    dv = x.shape[1]
    for h in range(x.shape[0] // QT):
        o_ref[:, h * dv:(h + 1) * dv] = x[h * QT:(h + 1) * QT].astype(o_ref.dtype)


def _lanes(x, n):
    return x if n == x.shape[1] else jnp.concatenate([x] * (n // x.shape[1]), axis=1)


def _rows(x, n):
    return x if n == x.shape[0] else jnp.concatenate([x] * (n // x.shape[0]), axis=0)


def _row_stat(x):
    return jnp.broadcast_to(x, (x.shape[0], 128))


def _key_block(ref, n, cols=slice(None)):
    return ref[pl.ds(pl.multiple_of(n * QT, QT), QT), cols]


def _flash_block(q, k, v, bias, scale, m_ref, l_ref, acc_ref):
    s = _dot_nt(q, k) * scale
    if bias is not None:
        s = s + bias
    m_prev = m_ref[...]
    m_new = jnp.maximum(m_prev, _row_stat(jnp.max(s, axis=1, keepdims=True)))
    alpha = jnp.exp(m_prev - m_new)
    p = jnp.exp(s - _lanes(m_new, s.shape[1]))
    l_ref[...] = alpha * l_ref[...] + _row_stat(jnp.sum(p, axis=1, keepdims=True))
    acc_ref[...] = _lanes(alpha, acc_ref.shape[1]) * acc_ref[...] + _dot(p.astype(BF16), v)
    m_ref[...] = m_new


def _flash_finish(o_ref, l_ref, acc_ref):
    _unstack_heads(o_ref, acc_ref[...] / _lanes(l_ref[...], acc_ref.shape[1]))


def _causal_bias():
    r = lax.broadcasted_iota(I32, (QT, QT), 0)
    c = lax.broadcasted_iota(I32, (QT, QT), 1)
    return jnp.where(c <= r, 0.0, NEG)


def _local_causal(strict):
    r = lax.broadcasted_iota(I32, (QT, QT), 0)
    c = lax.broadcasted_iota(I32, (QT, QT), 1)
    return jnp.where((c < r) if strict else (c <= r), 1.0, 0.0)


def _diag_mask(on_diag, local):
    return jnp.where(on_diag, local, 1.0) > 0.5


def _moba_prompt_body(q_ref, kvf_ref, kv_ref, o_ref, m_ref, l_ref, acc_ref):
    i = pl.program_id(1)
    nkb = kvf_ref.shape[0] // QT
    q = _stack_heads(q_ref, 128)
    qb = q.astype(BF16)
    rows = q.shape[0]
    kmean = jnp.concatenate(
        [jnp.mean(kvf_ref[n * QT:(n + 1) * QT, 0:128], axis=0, keepdims=True) for n in range(nkb)], axis=0)
    gate_t = _dot3_nt(kmean, q)
    blk = lax.broadcasted_iota(I32, gate_t.shape, 0)
    sel_t = _topk_rows(jnp.where(blk < i, gate_t, -jnp.inf), MOBA_TOPK, axis=0)
    drop = jnp.concatenate([1.0 - sel_t, jnp.zeros((128 - nkb, rows), F32)], axis=0).T
    q_aug = jnp.concatenate([qb, drop.astype(BF16)], axis=1)
    lane = lax.broadcasted_iota(I32, (QT, 128), 1)
    _flash_init(m_ref, l_ref, acc_ref)

    def past_block(n, carry):
        k_aug = jnp.concatenate([_key_block(kv_ref, n, slice(0, 128)),
                                 jnp.where(lane == n, NEG, 0.0).astype(BF16)], axis=1)
        _flash_block(q_aug, k_aug, _key_block(kv_ref, n, slice(128, 256)), None, ATTN_SCALE,
                     m_ref, l_ref, acc_ref)
        return carry

    lax.fori_loop(0, i, past_block, 0)
    _flash_block(qb, _key_block(kv_ref, i, slice(0, 128)), _key_block(kv_ref, i, slice(128, 256)),
                 _rows(_causal_bias(), rows), ATTN_SCALE, m_ref, l_ref, acc_ref)
    _flash_finish(o_ref, l_ref, acc_ref)


def _mla_prompt_body(q_ref, kv_ref, o_ref, m_ref, l_ref, acc_ref):
    i = pl.program_id(1)
    q = _stack_heads(q_ref, MLA_KV_PAD)
    _flash_init(m_ref, l_ref, acc_ref)

    def past_block(n, carry):
        kv = _key_block(kv_ref, n)
        _flash_block(q, kv, kv[:, 0:MLA_KV_LORA], None, MLA_SCALE, m_ref, l_ref, acc_ref)
        return carry

    lax.fori_loop(0, i, past_block, 0)
    kv = _key_block(kv_ref, i)
    _flash_block(q, kv, kv[:, 0:MLA_KV_LORA], _rows(_causal_bias(), q.shape[0]), MLA_SCALE,
                 m_ref, l_ref, acc_ref)
    _flash_finish(o_ref, l_ref, acc_ref)


def _sb_block(q, k, v, strict, tri, carry_ref, acc_ref):
    z = _dot_nt(q, k) * ATTN_SCALE
    log_keep = -_softplus(z)
    if strict is not None:
        log_keep = jnp.where(strict, log_keep, 0.0)
    hi, lo = _split_bf16(log_keep)
    later = (_dot(hi, tri) + _dot(lo, tri)) + _lanes(carry_ref[...], QT)
    a = jnp.exp((z + log_keep) + later)
    if strict is not None:
        a = jnp.where(strict, a, 0.0)
    acc_ref[...] += _dot(a.astype(BF16), v)
    carry_ref[...] += _row_stat(jnp.sum(log_keep, axis=1, keepdims=True))


def _sb_prompt_body(q_ref, kv_ref, tri_ref, o_ref, carry_ref, acc_ref):
    i = pl.program_id(1)
    qb = _stack_heads(q_ref, 128).astype(BF16)
    carry_ref[...] = jnp.zeros(carry_ref.shape, F32)
    acc_ref[...] = jnp.zeros(acc_ref.shape, F32)
    tri = tri_ref[...]
    strict = _rows(_local_causal(True), qb.shape[0]) > 0.5
    _sb_block(qb, _key_block(kv_ref, i, slice(0, 128)), _key_block(kv_ref, i, slice(128, 256)),
              strict, tri, carry_ref, acc_ref)

    def past_block(t, carry):
        n = i - 1 - t
        _sb_block(qb, _key_block(kv_ref, n, slice(0, 128)), _key_block(kv_ref, n, slice(128, 256)),
                  None, tri, carry_ref, acc_ref)
        return carry

    lax.fori_loop(0, i, past_block, 0)
    _unstack_heads(o_ref, acc_ref[...])


def _dsa_prompt_body(q_ref, kv_ref, bias_ref, o_ref, m_ref, l_ref, acc_ref):
    i = pl.program_id(1)
    qb = _stack_heads(q_ref, 128).astype(BF16)
    rows = qb.shape[0]
    _flash_init(m_ref, l_ref, acc_ref)

    def block(n, carry):
        _flash_block(qb, _key_block(kv_ref, n, slice(0, 128)), _key_block(kv_ref, n, slice(128, 256)),
                     _rows(bias_ref[n], rows), ATTN_SCALE, m_ref, l_ref, acc_ref)
        return carry

    lax.fori_loop(0, i + 1, block, 0)
    _flash_finish(o_ref, l_ref, acc_ref)


def prompt_attention(kind, q, kv, n_batch, seq, extra=None):
    nqt = seq // QT
    nkb = seq // QT
    hps = HEADS_PER_STEP
    rows = hps * QT
    dq = q.shape[1] // N_HEADS
    dkv = kv.shape[1]
    q_spec = pl.BlockSpec((QT, hps * dq), lambda b, i, g: (b * nqt + i, g))
    kv_spec = pl.BlockSpec((seq, dkv), lambda b, i, g: (b, 0))
    stat = pltpu.VMEM((rows, 128), F32)
    if kind == "moba":
        body, dv, ins, specs = _moba_prompt_body, 128, (q, extra, kv), [q_spec, kv_spec, kv_spec]
        scratch = [stat, stat, pltpu.VMEM((rows, dv), F32)]
    elif kind == "mla":
        body, dv, ins, specs = _mla_prompt_body, MLA_KV_LORA, (q, kv), [q_spec, kv_spec]
        scratch = [stat, stat, pltpu.VMEM((rows, dv), F32)]
    elif kind == "sb":
        body, dv, ins = _sb_prompt_body, 128, (q, kv, extra)
        specs = [q_spec, kv_spec, pl.BlockSpec((QT, QT), lambda b, i, g: (0, 0))]
        scratch = [stat, pltpu.VMEM((rows, dv), F32)]
    else:
        body, dv, ins = _dsa_prompt_body, 128, (q, kv, extra)
        specs = [q_spec, kv_spec, pl.BlockSpec((nkb, QT, QT), lambda b, i, g: (0, b * nqt + i, 0))]
        scratch = [stat, stat, pltpu.VMEM((rows, dv), F32)]
    return pl.pallas_call(
        body,
        out_shape=jax.ShapeDtypeStruct((n_batch * seq, N_HEADS * dv), BF16),
        grid=(n_batch, nqt, N_HEADS // hps),
        in_specs=specs,
        out_specs=pl.BlockSpec((QT, hps * dv), lambda b, i, g: (b * nqt + i, g)),
        scratch_shapes=scratch,
        compiler_params=_cparams(("parallel", "parallel", "arbitrary"), VMEM_LIMIT_MB),
        name=kind + "_prompt",
    )(*ins)


def _idx_prompt_body(iq_ref, ikd_ref, iw_ref, o_ref):
    i = pl.program_id(1)
    n = pl.program_id(2)

    @pl.when(n <= i)
    def _():
        ikd = ikd_ref[...]
        lane = lax.broadcasted_iota(I32, (QT, 128), 1)
        acc = jnp.zeros((QT, QT), F32)
        for j in range(IDX_HEADS // 2):
            qp = iq_ref[:, j * 128:(j + 1) * 128]
            zero = jnp.zeros_like(qp)
            le = _dot_nt(jnp.where(lane < IDX_DIM, qp, zero), ikd)
            lo = _dot_nt(jnp.where(lane >= IDX_DIM, qp, zero), ikd)
            acc = acc + iw_ref[:, 2 * j:2 * j + 1] * jnp.maximum(le, 0.0)
            acc = acc + iw_ref[:, 2 * j + 1:2 * j + 2] * jnp.maximum(lo, 0.0)
        o_ref[...] = jnp.where(_diag_mask(i == n, _local_causal(False)), acc, -jnp.inf)

    @pl.when(n > i)
    def _():
        o_ref[...] = jnp.full(o_ref.shape, -jnp.inf, F32)


def idx_prompt_scores(iq, ikd, iw, n_batch, seq):
    nqt = seq // QT
    return pl.pallas_call(
        _idx_prompt_body,
        out_shape=jax.ShapeDtypeStruct((n_batch * seq, seq), F32),
        grid=(n_batch, nqt, nqt),
        in_specs=[pl.BlockSpec((QT, 2048), lambda b, i, n: (b * nqt + i, 0)),
                  pl.BlockSpec((QT, 128), lambda b, i, n: (b * nqt + jnp.minimum(n, i), 0)),
                  pl.BlockSpec((QT, 128), lambda b, i, n: (b * nqt + i, 0))],
        out_specs=pl.BlockSpec((QT, QT), lambda b, i, n: (b * nqt + i, n)),
        compiler_params=_cparams(("parallel", "parallel", "arbitrary"), VMEM_LIMIT_MB),
        name="idx_prompt_scores",
    )(iq, ikd, iw)


def _topk_mask_body(s_ref, pos_ref, o_ref, *, k, idx_bits):
    s = s_ref[...]
    bits = lax.bitcast_convert_type(s, I32)
    key = jnp.where(bits >= 0, bits, bits ^ jnp.int32(0x7FFFFFFF))
    col = lax.broadcasted_iota(I32, s.shape, 1)
    kf = jnp.float32(k)

    def count(mask):
        return jnp.sum(jnp.where(mask, 1.0, 0.0), axis=1, keepdims=True)

    t0 = jnp.where(count(key >= 0) >= kf, jnp.int32(0), jnp.int32(INT_MIN))

    def value_step(b, t):
        cand = t | lax.shift_left(jnp.int32(1), 30 - b)
        return jnp.where(count(key >= cand) >= kf, cand, t)

    thr = lax.fori_loop(0, 31, value_step, t0)
    gt = key > thr
    eq = key == thr
    need = kf - count(gt)
    n_eq = count(eq)

    def tie_step(b, m):
        cand = m | lax.shift_left(jnp.int32(1), idx_bits - 1 - b)
        c = count(jnp.logical_and(eq, col < cand))
        return jnp.where(c < need, cand, m)

    def tie_search():
        return lax.fori_loop(0, idx_bits, tie_step, jnp.zeros(thr.shape, I32))

    def tie_all():
        return jnp.full(thr.shape, 2 ** idx_bits - 1, I32)

    last = lax.cond(jnp.max(n_eq - need) > 0.5, tie_search, tie_all)
    sel = jnp.logical_or(gt, jnp.logical_and(eq, col <= last))
    sel = jnp.logical_and(sel, col <= pos_ref[...])
    bias = jnp.where(sel, 0.0, NEG)
    if len(o_ref.shape) == 2:
        o_ref[...] = bias
    else:
        for n in range(o_ref.shape[0]):
            o_ref[n] = bias[:, n * QT:(n + 1) * QT]


def topk_bias(scores, pos, k, tr, blocked):
    r, l = scores.shape
    idx_bits = max(1, (l - 1).bit_length())
    if blocked:
        out_shape = jax.ShapeDtypeStruct((l // QT, r, QT), F32)
        out_spec = pl.BlockSpec((l // QT, tr, QT), lambda i: (0, i, 0))
    else:
        out_shape = jax.ShapeDtypeStruct((r, l), F32)
        out_spec = pl.BlockSpec((tr, l), lambda i: (i, 0))
    return pl.pallas_call(
        functools.partial(_topk_mask_body, k=k, idx_bits=idx_bits),
        out_shape=out_shape,
        grid=(r // tr,),
        in_specs=[pl.BlockSpec((tr, l), lambda i: (i, 0)),
                  pl.BlockSpec((tr, 1), lambda i: (i, 0))],
        out_specs=out_spec,
        compiler_params=_cparams(("parallel",), VMEM_LIMIT_MB),
        name="topk_bias",
    )(scores, pos)


def _paged_kernel(pt_ref, cache_ref, *refs, body, n_in, n_out, layer, n_pages, lane_pages):
    ins = refs[:n_in]
    outs = refs[n_in:n_in + n_out]
    buf, sem = refs[n_in + n_out:]
    r = pl.program_id(0)
    nr = pl.num_programs(0)
    page_rows, w = cache_ref.shape[2:]

    def page_copy(page, p, slot):
        if lane_pages:
            dst = buf.at[slot, :, pl.ds(p * w, w)]
        else:
            dst = buf.at[slot, pl.ds(p * page_rows, page_rows)]
        return pltpu.make_async_copy(cache_ref.at[layer, page], dst, sem.at[slot])

    def start_request(req, slot):
        for p in range(n_pages):
            page_copy(pt_ref[req * n_pages + p], p, slot).start()

    @pl.when(r == 0)
    def _():
        start_request(0, 0)

    @pl.when(r + 1 < nr)
    def _():
        start_request(r + 1, (r + 1) % 2)

    slot = r % 2
    for p in range(n_pages):
        page_copy(0, p, slot).wait()
    body(buf.at[slot], *ins, *outs)


def paged_call(name, body, cache, layer, page_table, ins, in_blocks, out_shape, out_block, lane_pages=False):
    n_req, n_pages = page_table.shape
    page_rows, w = cache.shape[2:]
    window = (page_rows, n_pages * w) if lane_pages else (n_pages * page_rows, w)
    nd = lambda blk: (lambda r, pt: (r,) + (0,) * (len(blk) - 1))
    const = lambda blk: (lambda r, pt: (0,) * len(blk))
    in_specs = [pl.BlockSpec(memory_space=pl.ANY)]
    for blk, per_req in in_blocks:
        in_specs.append(pl.BlockSpec(blk, nd(blk) if per_req else const(blk)))
    grid_spec = pltpu.PrefetchScalarGridSpec(
        num_scalar_prefetch=1,
        grid=(n_req,),
        in_specs=in_specs,
        out_specs=pl.BlockSpec(out_block, nd(out_block)),
        scratch_shapes=[pltpu.VMEM((2,) + window, F32),
                        pltpu.SemaphoreType.DMA((2,))])
    return pl.pallas_call(
        functools.partial(_paged_kernel, body=body, n_in=len(ins), n_out=1, layer=layer, n_pages=n_pages,
                          lane_pages=lane_pages),
        out_shape=out_shape,
        grid_spec=grid_spec,
        compiler_params=_cparams(("arbitrary",), VMEM_LIMIT_MB),
        name=name,
    )(page_table.reshape(-1), cache, *ins)


def _softmax_with_new(s, s_new, v, v_new, v_keys_on_lanes=False):
    m = jnp.maximum(jnp.max(s, axis=1, keepdims=True), s_new)
    p = jnp.exp(s - m)
    p_new = jnp.exp(s_new - m)
    l = jnp.sum(p, axis=1, keepdims=True) + p_new
    pv = _dot_nt(p.astype(BF16), v) if v_keys_on_lanes else _dot(p.astype(BF16), v)
    acc = pv + p_new.astype(BF16).astype(F32) * v_new.astype(BF16).astype(F32)
    return acc / l


def _score_new(q, k_new):
    return jnp.sum(q.astype(BF16).astype(F32) * k_new.astype(BF16).astype(F32), axis=1, keepdims=True)


def _kv_rows(win):
    n = win.shape[0] // 2
    return win[pl.ds(0, n, stride=2), :], win[pl.ds(1, n, stride=2), :]


def _moba_decode_body(win, q_ref, new_ref, e_ref, o_ref):
    q = q_ref[0]
    k, v = _kv_rows(win)
    v = v.astype(BF16)
    nb = k.shape[0] // MOBA_BLOCK
    kmean = jnp.mean(k.reshape(nb, MOBA_BLOCK, 128), axis=1)
    sel = _topk_rows(_dot3_nt(q, kmean), MOBA_TOPK)
    picked = _dot(sel.astype(BF16), e_ref[...]) > 0.5
    s = jnp.where(picked, _dot_nt(q.astype(BF16), k.astype(BF16)) * ATTN_SCALE, NEG)
    new = new_ref[0]
    s_new = _score_new(q, new[:, 0:128]) * ATTN_SCALE
    o_ref[0] = _softmax_with_new(s, s_new, v, new[:, 128:256])


def _mla_decode_body(win, q_ref, new_ref, o_ref):
    q = q_ref[0][:, 0:MLA_KV_LORA + MLA_ROPE]
    kvt = win[...].astype(BF16)
    s = _dot(q, kvt) * MLA_SCALE
    new = new_ref[0]
    s_new = _score_new(q.astype(F32), new) * MLA_SCALE
    o_ref[0] = _softmax_with_new(s, s_new, kvt[0:MLA_KV_LORA, :], new[:, 0:MLA_KV_LORA],
                                 v_keys_on_lanes=True).astype(o_ref.dtype)


def _sb_decode_body(win, q_ref, o_ref):
    q = q_ref[0].astype(BF16)
    k, v = _kv_rows(win)
    k = k.astype(BF16)
    v = v.astype(BF16)
    z = _dot_nt(q, k) * ATTN_SCALE
    log_keep = -_softplus(z)
    n = z.shape[1]
    col = lax.broadcasted_iota(I32, z.shape, 1)
    suffix = log_keep
    d = 1
    while d < n:
        suffix = suffix + jnp.where(col + d < n, pltpu.roll(suffix, n - d, 1), 0.0)
        d *= 2
    a = jnp.exp((z + log_keep) + (suffix - log_keep))
    o_ref[0] = _dot(a.astype(BF16), v)


def _idx_decode_body(win, iq_ref, iw_ref, new_ref, o_ref):
    iq = iq_ref[0]
    w = iw_ref[0]
    logits = _dot(iq, win[...].astype(BF16))
    score = jnp.sum(w * jnp.maximum(logits, 0.0), axis=0, keepdims=True)
    l_new = _score_new(iq.astype(F32), new_ref[0])
    s_new = jnp.sum(w * jnp.maximum(l_new, 0.0), axis=0, keepdims=True)
    n = score.shape[1]
    o_ref[0, :, 0:n] = score
    lane = lax.broadcasted_iota(I32, (1, 128), 1)
    o_ref[0, :, n:n + 128] = jnp.where(lane == 0, s_new, -jnp.inf)


def _dsa_decode_body(win, q_ref, new_ref, bias_ref, o_ref):
    q = q_ref[0]
    k, v = _kv_rows(win)
    k = k.astype(BF16)
    v = v.astype(BF16)
    n = k.shape[0]
    s = _dot_nt(q.astype(BF16), k) * ATTN_SCALE + bias_ref[0, :, 0:n]
    new = new_ref[0]
    s_new = _score_new(q, new[:, 0:128]) * ATTN_SCALE + bias_ref[0, :, n:n + 1]
    o_ref[0] = _softmax_with_new(s, s_new, v, new[:, 128:256])


def _rope_tables(pos):
    posf = pos.astype(F32)[:, None]
    inv128 = ROPE_THETA ** (-2.0 * jnp.arange(64, dtype=F32) / 128)
    ang = posf * inv128[None, :]
    c, s = jnp.cos(ang), jnp.sin(ang)
    tabs128 = (jnp.concatenate([c, c], axis=1), jnp.concatenate([-s, s], axis=1))
    inv64 = ROPE_THETA ** (-2.0 * jnp.arange(32, dtype=F32) / 64)
    ang = posf * inv64[None, :]
    c, s = jnp.cos(ang), jnp.sin(ang)
    z = jnp.zeros_like(s)
    tabs64 = (jnp.concatenate([c, c, c, c], axis=1),
              jnp.concatenate([-s, z, -s, z], axis=1),
              jnp.concatenate([z, s, z, s], axis=1))
    return tabs128, tabs64


def _pad_cols(w, n):
    return jnp.pad(w, ((0, 0), (0, n - w.shape[1])))


def _even_w_in(w):
    return _pad_cols(w, EVEN_PAD).astype(BF16)


def _odd_w_in(w):
    return jnp.concatenate([w[:, 0:6656], _pad_cols(w[:, 6656:6688], 128), _pad_cols(w[:, 6688:6752], 128)],
                           axis=1).astype(BF16)


def _mla_w_uq(w):
    w3 = w.reshape(w.shape[0], N_HEADS, MLA_QK)
    nope = w3[:, :, 0:128].reshape(w.shape[0], N_HEADS * 128)
    rope = jnp.pad(w3[:, :, 128:], ((0, 0), (0, 0), (0, 64))).reshape(w.shape[0], N_HEADS * 128)
    return jnp.concatenate([nope, rope], axis=1).astype(BF16)


def _block_membership(n_keys):
    blk = jnp.arange(n_keys, dtype=I32)[None, :] // MOBA_BLOCK
    return (blk == jnp.arange(n_keys // MOBA_BLOCK, dtype=I32)[:, None]).astype(BF16)


def _kv_row_view(cache):
    return cache.reshape(cache.shape[0], cache.shape[1], 2 * cache.shape[2], cache.shape[4])


def _even_mixer(stream, next_gain, j, dims, tabs, caches, page_table, w):
    n_batch, seq, n_req, tm, tr = dims
    n_prompt = n_batch * seq
    tabs128, tabs64 = tabs
    cache_moba, cache_mla = caches
    x = stream[0]
    p = matmul_normed(stream, _even_w_in(w["even_w_in"][j]), F32, tm // 2, EVEN_PAD // 7)
    mq, moba_new, moba_bf, cqn, mla_new, kv_pad = even_post(p, tabs128, tabs64, w["mla_q_norm_w"][j],
                                                            w["mla_kv_norm_w"][j], tr)
    qfull = matmul(cqn, _mla_w_uq(w["mla_w_uq"][j]), F32, tm, 512)
    w_uk_t = jnp.transpose(w["mla_w_uk"][j], (1, 2, 0)).astype(BF16)
    w_uv = jnp.transpose(w["mla_w_uv"][j], (1, 0, 2)).astype(BF16)
    q_cat = mla_qcat(qfull, w_uk_t, tabs64, tm)

    moba_p = prompt_attention("moba", mq, moba_bf, n_batch, seq, moba_new)
    mla_p = prompt_attention("mla", q_cat, kv_pad, n_batch, seq)

    n_past = page_table.shape[1] * PAGE_SIZE
    moba_s = paged_call(
        "moba_decode", _moba_decode_body, _kv_row_view(cache_moba), j, page_table,
        (mq[n_prompt:].reshape(n_req, N_HEADS, 128), moba_new[n_prompt:].reshape(n_req, 1, 256),
         _block_membership(n_past)),
        [((1, N_HEADS, 128), True), ((1, 1, 256), True), ((n_past // MOBA_BLOCK, n_past), False)],
        jax.ShapeDtypeStruct((n_req, N_HEADS, 128), F32), (1, N_HEADS, 128))
    mla_s = paged_call(
        "mla_decode", _mla_decode_body, jnp.swapaxes(cache_mla, 2, 3), j, page_table,
        (q_cat[n_prompt:].reshape(n_req, N_HEADS, MLA_KV_PAD), mla_new[n_prompt:].reshape(n_req, 1, -1)),
        [((1, N_HEADS, MLA_KV_PAD), True), ((1, 1, MLA_KV_LORA + MLA_ROPE), True)],
        jax.ShapeDtypeStruct((n_req, N_HEADS, MLA_KV_LORA), BF16), (1, N_HEADS, MLA_KV_LORA),
        lane_pages=True)

    moba_out = jnp.concatenate([moba_p, moba_s.reshape(n_req, -1).astype(BF16)], axis=0)
    mla_lat = jnp.concatenate([mla_p, mla_s.reshape(n_req, -1)], axis=0)
    mla_out = headwise_matmul(mla_lat, w_uv, BF16, tm)
    stream = matmul2_residual(moba_out, mla_out, w["even_w_out"], (j,), x, next_gain, tm, 256)
    return stream, (moba_new, mla_new)


def _odd_mixer(stream, next_gain, j, dims, tabs, caches, page_table, w, consts):
    n_batch, seq, n_req, tm, tr = dims
    n_prompt = n_batch * seq
    tabs128, tabs64 = tabs
    cache_sb, cache_dsa, cache_idx = caches
    tri, pos_prompt, pos_sample = consts
    x = stream[0]
    p = matmul_normed(stream, _odd_w_in(w["odd_w_in"][j]), F32, tm // 2, ODD_PAD // 9)
    sq, sb_new, sb_bf, dq, dsa_new, dsa_bf, iq, iw, idx_new, ikd = odd_post(p, tabs128, tabs64, tr)

    sb_p = prompt_attention("sb", sq, sb_bf, n_batch, seq, tri)
    scores_p = idx_prompt_scores(iq, ikd, iw, n_batch, seq)
    bias_p = topk_bias(scores_p, pos_prompt, min(DSA_TOPK, seq // 4), QT, blocked=True)
    dsa_p = prompt_attention("dsa", dq, dsa_bf, n_batch, seq, bias_p)

    n_past = page_table.shape[1] * PAGE_SIZE
    sb_s = paged_call(
        "sb_decode", _sb_decode_body, _kv_row_view(cache_sb), j, page_table,
        (sq[n_prompt:].reshape(n_req, N_HEADS, 128),),
        [((1, N_HEADS, 128), True)],
        jax.ShapeDtypeStruct((n_req, N_HEADS, 128), F32), (1, N_HEADS, 128))
    scores_s = paged_call(
        "idx_decode", _idx_decode_body, jnp.swapaxes(cache_idx, 2, 3), j, page_table,
        (iq[n_prompt:].reshape(n_req, IDX_HEADS, IDX_DIM), iw[n_prompt:, 0:IDX_HEADS].reshape(n_req, IDX_HEADS, 1),
         idx_new[n_prompt:].reshape(n_req, 1, IDX_DIM)),
        [((1, IDX_HEADS, IDX_DIM), True), ((1, IDX_HEADS, 1), True), ((1, 1, IDX_DIM), True)],
        jax.ShapeDtypeStruct((n_req, 1, n_past + 128), F32), (1, 1, n_past + 128), lane_pages=True)
    bias_s = topk_bias(scores_s.reshape(n_req, n_past + 128), pos_sample, min(DSA_TOPK, (n_past + 1) // 4),
                       n_req, blocked=False)
    dsa_s = paged_call(
        "dsa_decode", _dsa_decode_body, _kv_row_view(cache_dsa), j, page_table,
        (dq[n_prompt:].reshape(n_req, N_HEADS, 128), dsa_new[n_prompt:].reshape(n_req, 1, 256),
         bias_s.reshape(n_req, 1, n_past + 128)),
        [((1, N_HEADS, 128), True), ((1, 1, 256), True), ((1, 1, n_past + 128), True)],
        jax.ShapeDtypeStruct((n_req, N_HEADS, 128), F32), (1, N_HEADS, 128))

    sb_out = jnp.concatenate([sb_p, sb_s.reshape(n_req, -1).astype(BF16)], axis=0)
    dsa_out = jnp.concatenate([dsa_p, dsa_s.reshape(n_req, -1).astype(BF16)], axis=0)
    stream = matmul2_residual(sb_out, dsa_out, w["odd_w_out"], (j,), x, next_gain, tm, 256)
    return stream, (sb_new, dsa_new, idx_new)


def _trunk(x_prompt, x_sample, caches, page_table, w, tm, tr):
    n_batch, seq, d = x_prompt.shape
    n_req = x_sample.shape[0]
    n_prompt = n_batch * seq
    n_past = page_table.shape[1] * PAGE_SIZE
    depth = w["norm_w"].shape[0]
    dims = (n_batch, seq, n_req, tm, tr)
    x = jnp.concatenate([x_prompt.reshape(n_prompt, d), x_sample.reshape(n_req, d)], axis=0)
    pos = jnp.concatenate([jnp.tile(jnp.arange(seq, dtype=I32), n_batch), jnp.full((n_req,), n_past, I32)])
    tabs = _rope_tables(pos)
    tri = (jnp.arange(QT)[:, None] > jnp.arange(QT)[None, :]).astype(BF16)
    consts = (tri, pos[:n_prompt, None], pos[n_prompt:, None])
    cache_moba, cache_mla, cache_sb, cache_dsa, cache_idx = caches
    states = []
    tn = 256
    for layer in range(depth):
        j = layer // 2
        nw = w["norm_w"][layer]
        if layer == 0:
            stream = (x,) + tuple(stream_prep(x, nw[0], tr))
        after = w["norm_w"][layer + 1, 0] if layer + 1 < depth else w["final_norm_w"]
        a = ffn_up(stream, w["ffn_w_gate"], w["ffn_w_up"], (layer, 0), tm, tn)
        stream = matmul_residual(a, w["ffn_w_down"], (layer, 0), stream[0], 0.5, nw[1], tm, tn)
        if layer % 2 == 0:
            stream, st = _even_mixer(stream, nw[2], j, dims, tabs, (cache_moba, cache_mla), page_table, w)
        else:
            stream, st = _odd_mixer(stream, nw[2], j, dims, tabs, (cache_sb, cache_dsa, cache_idx), page_table,
                                    w, consts)
        states.append(st)
        a = ffn_up(stream, w["ffn_w_gate"], w["ffn_w_up"], (layer, 1), tm, tn)
        stream = matmul_residual(a, w["ffn_w_down"], (layer, 1), stream[0], 0.5, after, tm, tn)
    y = rmsnorm(stream[0], w["final_norm_w"], F32, tr)

    def split(rows, tail):
        a = jnp.stack(rows)
        return (a[:, :n_prompt].reshape(a.shape[0], n_batch, seq, *tail),
                a[:, n_prompt:].reshape(a.shape[0], n_req, 1, *tail))

    even = states[0::2]
    odd = states[1::2]
    moba_p, moba_s = split([s[0] for s in even], (2, HEAD_DIM))
    mla_p, mla_s = split([s[1] for s in even], (MLA_KV_LORA + MLA_ROPE,))
    sb_p, sb_s = split([s[0] for s in odd], (2, HEAD_DIM))
    dsa_p, dsa_s = split([s[1] for s in odd], (2, HEAD_DIM))
    idx_p, idx_s = split([s[2] for s in odd], (IDX_DIM,))
    return (y[:n_prompt].reshape(n_batch, seq, d), y[n_prompt:].reshape(n_req, 1, d),
            moba_p, moba_s, mla_p, mla_s, sb_p, sb_s, dsa_p, dsa_s, idx_p, idx_s)


def kernel(x_prompt, x_sample, cache_moba_kv, cache_mla_latent, cache_sb_kv, cache_dsa_kv, cache_dsa_idx_k, page_table, norm_w, ffn_w_gate, ffn_w_up, ffn_w_down, even_w_in, even_w_out, mla_q_norm_w, mla_w_uq, mla_kv_norm_w, mla_w_uk, mla_w_uv, odd_w_in, odd_w_out, final_norm_w):
    w = dict(norm_w=norm_w, ffn_w_gate=ffn_w_gate, ffn_w_up=ffn_w_up, ffn_w_down=ffn_w_down,
             even_w_in=even_w_in, even_w_out=even_w_out, mla_q_norm_w=mla_q_norm_w, mla_w_uq=mla_w_uq,
             mla_kv_norm_w=mla_kv_norm_w, mla_w_uk=mla_w_uk, mla_w_uv=mla_w_uv,
             odd_w_in=odd_w_in, odd_w_out=odd_w_out, final_norm_w=final_norm_w)
    caches = (cache_moba_kv, cache_mla_latent, cache_sb_kv, cache_dsa_kv, cache_dsa_idx_k)
    n_tokens = x_prompt.shape[0] * x_prompt.shape[1] + x_sample.shape[0]
    return _trunk(x_prompt, x_sample, caches, page_table, w, n_tokens // 4, n_tokens // 40)
```
